```python
import jax, jax.numpy as jnp
from jax import lax
import numpy as np

D_MODEL = 2048
BATCH = 2
SEQ = 8192
DEPTH = 1
DEC_BATCH = 128
DEC_SEQ = 8
PAST_LEN = 16384
PAGE_SIZE = 128

HEAD_DIM = 64
ROT_DIM = HEAD_DIM // 4
ROPE_THETA = 500000.0
BLK = 128
A_WINDOW = 128
A_Q_HEADS = 16
A_KV_HEADS = 2
A_GROUP = A_Q_HEADS // A_KV_HEADS
B_PAIRS = ((128, 1), (512, 4), (2048, 16))
B_Q_HEADS = 8
B_KV_HEADS = 4
B_GROUP = B_Q_HEADS // B_KV_HEADS
A_WIDTH = A_Q_HEADS * HEAD_DIM
B_WIDTH = B_Q_HEADS * HEAD_DIM
A_KV_WIDTH = A_KV_HEADS * HEAD_DIM
B_KV_WIDTH = B_KV_HEADS * HEAD_DIM
SIZES = ([A_WIDTH, A_KV_WIDTH, A_KV_WIDTH, A_WIDTH]
         + [B_WIDTH, B_KV_WIDTH, B_KV_WIDTH] * len(B_PAIRS)
         + [B_WIDTH, D_MODEL, D_MODEL])
IN_WIDTH = sum(SIZES)
SPLIT_POINTS = tuple(int(c) for c in np.cumsum(SIZES)[:-1])
SCALE = HEAD_DIM ** -0.5
DN_ALPHA = (2 * DEPTH) ** 0.25
DN_BETA = (8 * DEPTH) ** -0.25
LN_EPS = 1e-5

kernel_name = 'hybrid_gated_swa_dilated_step'


def _rope(x, pos):
    half = ROT_DIM // 2
    inv = ROPE_THETA ** (-jnp.arange(0, ROT_DIM, 2, dtype=jnp.float32) / ROT_DIM)
    ang = pos[:, None] * inv[None, :]
    shape = (1, pos.shape[0]) + (1,) * (x.ndim - 3) + (half,)
    cos = jnp.cos(ang).reshape(shape)
    sin = jnp.sin(ang).reshape(shape)
    xr = x[..., :ROT_DIM].astype(jnp.float32)
    x1, x2 = xr[..., :half], xr[..., half:]
    rot = jnp.concatenate([x1 * cos - x2 * sin, x2 * cos + x1 * sin], axis=-1).astype(x.dtype)
    return jnp.concatenate([rot, x[..., ROT_DIM:]], axis=-1)


def _masked_softmax(s, valid, sink):
    s = jnp.where(valid, s, -jnp.inf)
    m = jnp.max(s, axis=-1, keepdims=True)
    if sink is not None:
        m = jnp.maximum(m, sink)
    p = jnp.exp(s - m)
    den = jnp.sum(p, axis=-1, keepdims=True)
    if sink is not None:
        den = den + jnp.exp(sink - m)
    return p / den, (m + jnp.log(den))[..., 0]


def _banded_attention(q, k, v, max_dist, sink):
    n, l, hk, g, d = q.shape
    nb = -(-l // BLK)
    pad = nb * BLK - l
    q = jnp.pad(q, ((0, 0), (0, pad), (0, 0), (0, 0), (0, 0)))
    k = jnp.pad(k, ((0, 0), (BLK, pad), (0, 0), (0, 0)))
    v = jnp.pad(v, ((0, 0), (BLK, pad), (0, 0), (0, 0)))
    qb = q.reshape(n, nb, BLK, hk, g, d)
    kb = k.reshape(n, nb + 1, BLK, hk, d)
    vb = v.reshape(n, nb + 1, BLK, hk, d)
    kw = jnp.concatenate([kb[:, :-1], kb[:, 1:]], axis=2)
    vw = jnp.concatenate([vb[:, :-1], vb[:, 1:]], axis=2)
    s = jnp.einsum('nbqhgd,nbkhd->nbhgqk', qb, kw).astype(jnp.float32) * SCALE
    qi = np.arange(BLK)[:, None]
    kj = np.arange(2 * BLK)[None, :]
    dist = BLK + qi - kj
    keypos = (np.arange(nb)[:, None, None] - 1) * BLK + kj[None]
    valid = (dist >= 0) & (dist <= max_dist) & (keypos >= 0)
    p, lse = _masked_softmax(s, jnp.asarray(valid)[None, :, None, None], sink)
    o = jnp.einsum('nbhgqk,nbkhd->nbqhgd', p.astype(v.dtype), vw)
    o = o.reshape(n, nb * BLK, hk, g, d)[:, :l]
    lse = jnp.transpose(lse, (0, 1, 4, 2, 3)).reshape(n, nb * BLK, hk, g)[:, :l]
    return o, lse


def _gathered_attention(q, k, v, start, q_pos, dists, sink):
    kpos = q_pos[:, None] - dists[None, :]
    valid = jnp.asarray(kpos >= 0)
    idx = jnp.asarray(np.clip(kpos - start, 0, k.shape[1] - 1))
    kg = k[:, idx]
    vg = v[:, idx]
    s = jnp.einsum('nthgd,ntjhd->nthgj', q, kg).astype(jnp.float32) * SCALE
    p, lse = _masked_softmax(s, valid[None, :, None, None, :], sink)
    o = jnp.einsum('nthgj,ntjhd->nthgd', p.astype(v.dtype), vg)
    return o, lse


def _fold(x, d):
    n, l = x.shape[:2]
    rest = x.shape[2:]
    x = x.reshape((n, l // d, d) + rest)
    return jnp.swapaxes(x, 1, 2).reshape((n * d, l // d) + rest)


def _unfold(x, n, d):
    m = x.shape[1]
    rest = x.shape[2:]
    x = x.reshape((n, d, m) + rest)
    return jnp.swapaxes(x, 1, 2).reshape((n, m * d) + rest)


def _combine_dilations(outs, lses):
    w = jax.nn.softmax(jnp.stack(lses, 0), axis=0)
    o = jnp.stack(outs, 0).astype(jnp.float32)
    return jnp.sum(w[..., None] * o, axis=0).astype(outs[0].dtype)


def _in_proj(x, w_in, b_in):
    n, l, _ = x.shape
    parts = jnp.split(x @ w_in + b_in, SPLIT_POINTS, axis=-1)
    qa = parts[0].reshape(n, l, A_KV_HEADS, A_GROUP, HEAD_DIM)
    ka = parts[1].reshape(n, l, A_KV_HEADS, HEAD_DIM)
    va = parts[2].reshape(n, l, A_KV_HEADS, HEAD_DIM)
    za = parts[3]
    qb = [parts[4 + 3 * i].reshape(n, l, B_KV_HEADS, B_GROUP, HEAD_DIM) for i in range(len(B_PAIRS))]
    kb = [parts[5 + 3 * i].reshape(n, l, B_KV_HEADS, HEAD_DIM) for i in range(len(B_PAIRS))]
    vb = [parts[6 + 3 * i].reshape(n, l, B_KV_HEADS, HEAD_DIM) for i in range(len(B_PAIRS))]
    zb, ga, gb = parts[-3], parts[-2], parts[-1]
    return qa, ka, va, za, qb, kb, vb, zb, ga, gb


def _out_proj(x, oa, ob, za, zb, ga, gb, w_br_a, w_br_b, w_out, ln_g, ln_b):
    n, l, _ = x.shape
    ya = oa.reshape(n, l, A_WIDTH) * jax.nn.silu(za)
    yb = ob.reshape(n, l, B_WIDTH) * jax.nn.silu(zb)
    m = jax.nn.sigmoid(ga) * (ya @ w_br_a) + jax.nn.sigmoid(gb) * (yb @ w_br_b)
    h = (DN_ALPHA * x + m @ w_out).astype(jnp.float32)
    mu = jnp.mean(h, axis=-1, keepdims=True)
    var = jnp.mean(jnp.square(h - mu), axis=-1, keepdims=True)
    return ((h - mu) * lax.rsqrt(var + LN_EPS) * ln_g + ln_b).astype(x.dtype)


def _prompt_layer(x, w_in, b_in, sink_a, w_br_a, w_br_b, w_out, ln_g, ln_b):
    n, l, _ = x.shape
    pos = jnp.arange(l, dtype=jnp.float32)
    qa, ka, va, za, qb, kb, vb, zb, ga, gb = _in_proj(x, w_in, b_in)
    qa, ka = _rope(qa, pos), _rope(ka, pos)
    oa, _ = _banded_attention(qa, ka, va, A_WINDOW - 1, sink_a.reshape(A_KV_HEADS, A_GROUP, 1, 1))
    wa = min(A_WINDOW, l)
    states = [jnp.stack([ka, va], axis=2)[:, l - wa:]]
    outs, lses = [], []
    for (win, dil), q, k, v in zip(B_PAIRS, qb, kb, vb):
        q, k = _rope(q, pos), _rope(k, pos)
        o, lse = _banded_attention(_fold(q, dil), _fold(k, dil), _fold(v, dil), win // dil, None)
        outs.append(_unfold(o, n, dil))
        lses.append(_unfold(lse, n, dil))
        wb = min(win, l)
        states.append(jnp.stack([k, v], axis=2)[:, l - wb:])
    ob = _combine_dilations(outs, lses)
    y = _out_proj(x, oa, ob, za, zb, ga, gb, w_br_a, w_br_b, w_out, ln_g, ln_b)
    return y, states


def _sample_layer(x, cache_a, caches_b, w_in, b_in, sink_a, w_br_a, w_br_b, w_out, ln_g, ln_b):
    n, t, _ = x.shape
    pos = PAST_LEN + jnp.arange(t, dtype=jnp.float32)
    q_pos = PAST_LEN + np.arange(t)
    qa, ka, va, za, qb, kb, vb, zb, ga, gb = _in_proj(x, w_in, b_in)
    qa, ka = _rope(qa, pos), _rope(ka, pos)
    ca = cache_a.shape[1]
    kv_a = jnp.concatenate([cache_a, jnp.stack([ka, va], axis=2)], axis=1)
    oa, _ = _gathered_attention(qa, kv_a[:, :, 0], kv_a[:, :, 1], PAST_LEN - ca, q_pos,
                                np.arange(A_WINDOW), sink_a.reshape(A_KV_HEADS, A_GROUP, 1))
    states = [kv_a[:, -ca:]]
    outs, lses = [], []
    for (win, dil), cache, q, k, v in zip(B_PAIRS, caches_b, qb, kb, vb):
        q, k = _rope(q, pos), _rope(k, pos)
        cb = cache.shape[1]
        kv = jnp.concatenate([cache, jnp.stack([k, v], axis=2)], axis=1)
        o, lse = _gathered_attention(q, kv[:, :, 0], kv[:, :, 1], PAST_LEN - cb, q_pos,
                                     np.arange(0, win + 1, dil), None)
        outs.append(o)
        lses.append(lse)
        states.append(kv[:, -cb:])
    ob = _combine_dilations(outs, lses)
    y = _out_proj(x, oa, ob, za, zb, ga, gb, w_br_a, w_br_b, w_out, ln_g, ln_b)
    return y, states


def setup_inputs(seed: int = 0) -> dict:
    key = jax.random.key(seed)
    ks = jax.random.split(key, 16)
    f32 = jnp.float32
    nrm = lambda k, shape: jax.random.normal(k, shape, f32)
    la = min(A_WINDOW, PAST_LEN)
    lb = [min(w, PAST_LEN) for w, _ in B_PAIRS]
    return {
        'x_prompt': nrm(ks[0], (BATCH, SEQ, D_MODEL)),
        'x_sample': nrm(ks[1], (DEC_BATCH, DEC_SEQ, D_MODEL)),
        'cache_a_kv': nrm(ks[2], (DEPTH, DEC_BATCH, la, 2, A_KV_HEADS, HEAD_DIM)),
        'cache_b0_kv': nrm(ks[3], (DEPTH, DEC_BATCH, lb[0], 2, B_KV_HEADS, HEAD_DIM)),
        'cache_b1_kv': nrm(ks[4], (DEPTH, DEC_BATCH, lb[1], 2, B_KV_HEADS, HEAD_DIM)),
        'cache_b2_kv': nrm(ks[5], (DEPTH, DEC_BATCH, lb[2], 2, B_KV_HEADS, HEAD_DIM)),
        'w_in': nrm(ks[6], (DEPTH, D_MODEL, IN_WIDTH)) * D_MODEL ** -0.5,
        'b_in': nrm(ks[7], (DEPTH, IN_WIDTH)) * 0.02,
        'sink_a': nrm(ks[8], (DEPTH, A_Q_HEADS)) * 0.5,
        'w_br_a': nrm(ks[9], (DEPTH, A_WIDTH, D_MODEL)) * (A_WIDTH ** -0.5 * DN_BETA),
        'w_br_b': nrm(ks[10], (DEPTH, B_WIDTH, D_MODEL)) * (B_WIDTH ** -0.5 * DN_BETA),
        'w_out': nrm(ks[11], (DEPTH, D_MODEL, D_MODEL)) * (D_MODEL ** -0.5 * DN_BETA),
        'ln_g': 1.0 + 0.02 * nrm(ks[12], (DEPTH, D_MODEL)),
        'ln_b': 0.02 * nrm(ks[13], (DEPTH, D_MODEL)),
    }


def reference(x_prompt, x_sample, cache_a_kv, cache_b0_kv, cache_b1_kv, cache_b2_kv,
              w_in, b_in, sink_a, w_br_a, w_br_b, w_out, ln_g, ln_b):
    hp, hs = x_prompt, x_sample
    prompt_st = [[], [], [], []]
    sample_st = [[], [], [], []]
    for layer in range(DEPTH):
        hp, sp = _prompt_layer(hp, w_in[layer], b_in[layer], sink_a[layer], w_br_a[layer],
                               w_br_b[layer], w_out[layer], ln_g[layer], ln_b[layer])
        hs, ss = _sample_layer(hs, cache_a_kv[layer],
                               (cache_b0_kv[layer], cache_b1_kv[layer], cache_b2_kv[layer]),
                               w_in[layer], b_in[layer], sink_a[layer], w_br_a[layer],
                               w_br_b[layer], w_out[layer], ln_g[layer], ln_b[layer])
        for lst, s in zip(prompt_st, sp):
            lst.append(s)
        for lst, s in zip(sample_st, ss):
            lst.append(s)
    prompt_a_kv, prompt_b0_kv, prompt_b1_kv, prompt_b2_kv = [jnp.stack(s, 0) for s in prompt_st]
    sample_a_kv, sample_b0_kv, sample_b1_kv, sample_b2_kv = [jnp.stack(s, 0) for s in sample_st]
    return (hp, hs, prompt_a_kv, prompt_b0_kv, prompt_b1_kv, prompt_b2_kv,
            sample_a_kv, sample_b0_kv, sample_b1_kv, sample_b2_kv)
```

```python
import functools

import jax
import jax.numpy as jnp
from jax import lax
from jax.experimental import pallas as pl
from jax.experimental.pallas import tpu as pltpu

D_MODEL = 2048
SEQ = 8192
DEC_SEQ = 8
PAST_LEN = 16384
HEAD_DIM = 64
ROT_DIM = HEAD_DIM // 4
ROPE_THETA = 500000.0
BLK = 128
A_WINDOW = 128
A_Q_HEADS = 16
A_KV_HEADS = 2
B_PAIRS = ((128, 1), (512, 4), (2048, 16))
B_Q_HEADS = 8
B_KV_HEADS = 4
A_WIDTH = A_Q_HEADS * HEAD_DIM
B_WIDTH = B_Q_HEADS * HEAD_DIM
A_KV_WIDTH = A_KV_HEADS * HEAD_DIM
B_KV_WIDTH = B_KV_HEADS * HEAD_DIM
SCALE = HEAD_DIM ** -0.5
DN_ALPHA = 2.0 ** 0.25
LN_EPS = 1e-5
LANES = 128
HALF = LANES // 2

_ORIG = {}
_off = 0
for _name, _w in ([("qa", A_WIDTH), ("ka", A_KV_WIDTH), ("va", A_KV_WIDTH), ("za", A_WIDTH)]
                  + [(f"{p}b{i}", w) for i in range(3)
                     for p, w in (("q", B_WIDTH), ("k", B_KV_WIDTH), ("v", B_KV_WIDTH))]
                  + [("zb", B_WIDTH), ("ga", D_MODEL), ("gb", D_MODEL)]):
    _ORIG[_name] = (_off, _w)
    _off += _w
IN_WIDTH = _off
N_SLABS = IN_WIDTH // LANES

_SEGS = ["qa", "qb0", "qb1", "qb2", "kb0", "kb1", "kb2", "ka", "va",
         "vb0", "vb1", "ga", "gb", "za", "zb", "vb2"]
_SLAB = {}
_off = 0
for _name in _SEGS:
    _SLAB[_name] = _off // LANES
    assert _SLAB[_name] % min(_ORIG[_name][1] // LANES, 16) == 0 or _name in ("ka", "va")
    _off += _ORIG[_name][1]
assert _off == IN_WIDTH and _SLAB["va"] == _SLAB["ka"] + 1 and _SLAB["ka"] % 2 == 0
N_ROPE_SLABS = _SLAB["va"]

_VMEM_LIMIT = 56 * 1024 * 1024


def _params(n_axes, vmem=_VMEM_LIMIT):
    return pltpu.CompilerParams(dimension_semantics=("arbitrary",) * n_axes,
                                vmem_limit_bytes=vmem)


def _rope_tables(pos):
    half = ROT_DIM // 2
    inv = ROPE_THETA ** (-jnp.arange(0, ROT_DIM, 2, dtype=jnp.float32) / ROT_DIM)
    ang = pos[:, None] * inv[None, :]
    cos, sin = jnp.cos(ang), jnp.sin(ang)
    p = pos.shape[0]
    ones = jnp.ones((p, HEAD_DIM - ROT_DIM), jnp.float32)
    zeros_h = jnp.zeros((p, half), jnp.float32)
    zeros_r = jnp.zeros((p, HEAD_DIM - ROT_DIM), jnp.float32)
    c = jnp.concatenate([cos, cos, ones], axis=1)
    s1 = jnp.concatenate([-sin, zeros_h, zeros_r], axis=1)
    s2 = jnp.concatenate([zeros_h, sin, zeros_r], axis=1)
    tile = lambda t: jnp.concatenate([t, t], axis=1)
    return tile(c), tile(s1), tile(s2)


def _rope(h, c, s1, s2):
    half = ROT_DIM // 2
    return h * c + pltpu.roll(h, LANES - half, 1) * s1 + pltpu.roll(h, half, 1) * s2


def _to_half(piece, src_half, dst_half, lo):
    if src_half != dst_half:
        piece = pltpu.roll(piece, HALF, 1)
    keep = lo if dst_half == 0 else jnp.logical_not(lo)
    return jnp.where(keep, piece, 0.0)


def _merge_halves(even, odd, src_half, lo):
    if src_half == 1:
        even = pltpu.roll(even, HALF, 1)
    else:
        odd = pltpu.roll(odd, HALF, 1)
    return jnp.where(lo, even, odd)


def _nt(a, b):
    return lax.dot_general(a, b, (((1,), (1,)), ((), ())), preferred_element_type=jnp.float32)


def _band_mask(step, lo_off, hi_off):
    i = lax.broadcasted_iota(jnp.int32, (BLK, 2 * BLK), 0)
    j = lax.broadcasted_iota(jnp.int32, (BLK, 2 * BLK), 1)
    j = j ^ jnp.where(step % 2 == 0, BLK, 0)
    first_key = jnp.where(step == 0, BLK, 0)
    return (j >= i + lo_off) & (j <= i + hi_off) & (j >= first_key)


def _cur_half(step):
    return pl.ds(pl.multiple_of((step % 2) * BLK, BLK), BLK)


def _in_proj_kernel(x_ref, w_ref, b_ref, c_ref, s1_ref, s2_ref, o_ref, xb_ref, *, slabs):
    j = pl.program_id(1)

    @pl.when(j == 0)
    def _():
        xb_ref[...] = x_ref[...].astype(jnp.bfloat16)

    acc = jnp.dot(xb_ref[...], w_ref[...], preferred_element_type=jnp.float32) + b_ref[...]
    for s in range(slabs):
        col = acc[:, s * LANES:(s + 1) * LANES]
        is_rope = j * slabs + s < N_ROPE_SLABS

        @pl.when(is_rope)
        def _():
            o_ref[s] = _rope(col, c_ref[...], s1_ref[...], s2_ref[...])

        @pl.when(jnp.logical_not(is_rope))
        def _():
            o_ref[s] = col


def _in_proj(x, w_bf, b, tabs, tm=1024, tn=768):
    m, k = x.shape
    n = w_bf.shape[1]
    n_pos_blocks = tabs[0].shape[0] // tm
    tab = pl.BlockSpec((tm, LANES), lambda i, j: (i % n_pos_blocks, 0))
    return pl.pallas_call(
        functools.partial(_in_proj_kernel, slabs=tn // LANES),
        grid=(m // tm, n // tn),
        in_specs=[pl.BlockSpec((tm, k), lambda i, j: (i, 0)),
                  pl.BlockSpec((k, tn), lambda i, j: (0, j)),
                  pl.BlockSpec((1, tn), lambda i, j: (0, j)), tab, tab, tab],
        out_specs=pl.BlockSpec((tn // LANES, tm, LANES), lambda i, j: (j, i, 0)),
        out_shape=jax.ShapeDtypeStruct((n // LANES, m, LANES), jnp.float32),
        scratch_shapes=[pltpu.VMEM((tm, k), jnp.bfloat16)],
        compiler_params=_params(2),
        name="in_proj",
    )(x, w_bf, b, *tabs)


def _attn_a_kernel(sink_ref, q_ref, kv_ref, o_ref, st_ref, k_scr, v_scr, s_scr, p_scr, *,
                   n_blocks):
    b = pl.program_id(1)

    @pl.when(b == 0)
    def _():
        k_scr[BLK:2 * BLK, :] = jnp.zeros((BLK, LANES), jnp.bfloat16)
        v_scr[BLK:2 * BLK, :] = jnp.zeros((BLK, LANES), jnp.bfloat16)

    k_scr[_cur_half(b), :] = kv_ref[0].astype(jnp.bfloat16)
    v_scr[_cur_half(b), :] = kv_ref[1].astype(jnp.bfloat16)

    @pl.when(b == n_blocks - 1)
    def _():
        st_ref[0, :, 0:LANES] = kv_ref[0]
        st_ref[0, :, LANES:2 * LANES] = kv_ref[1]

    lo = lax.broadcasted_iota(jnp.int32, (BLK, LANES), 1) < HALF
    group = A_Q_HEADS // A_KV_HEADS
    pieces = []
    for pair in range(A_Q_HEADS // 2):
        slab = q_ref[pair] * SCALE
        for sub in range(2):
            g = (2 * pair + sub) // group
            pieces.append(_to_half(slab, sub, g, lo).astype(jnp.bfloat16))
    qz = jnp.concatenate(pieces, axis=0)
    s_scr[...] = _nt(qz, k_scr[...])

    mask = _band_mask(b, 1, A_WINDOW)
    inv = []
    for h in range(A_Q_HEADS):
        s = jnp.where(mask, s_scr[h * BLK:(h + 1) * BLK, :], -jnp.inf)
        sink = sink_ref[h]
        m = jnp.maximum(jnp.max(s, axis=1, keepdims=True), sink)
        p = jnp.exp(s - m)
        den = jnp.sum(p, axis=1, keepdims=True) + jnp.exp(sink - m)
        p_scr[h * BLK:(h + 1) * BLK, :] = p.astype(jnp.bfloat16)
        inv.append(1.0 / den)
    o_all = jnp.dot(p_scr[...], v_scr[...], preferred_element_type=jnp.float32)
    for pair in range(A_Q_HEADS // 2):
        e = o_all[(2 * pair) * BLK:(2 * pair + 1) * BLK, :] * inv[2 * pair]
        o = o_all[(2 * pair + 1) * BLK:(2 * pair + 2) * BLK, :] * inv[2 * pair + 1]
        o_ref[pair] = _merge_halves(e, o, (2 * pair) // group, lo)


def _attn_a(h3, sink, n, l):
    nb = l // BLK
    q_slabs = A_WIDTH // LANES
    return pl.pallas_call(
        functools.partial(_attn_a_kernel, n_blocks=nb),
        grid=(n, nb),
        in_specs=[pl.BlockSpec(memory_space=pltpu.SMEM),
                  pl.BlockSpec((q_slabs, BLK, LANES),
                               lambda i, b: (_SLAB["qa"] // q_slabs, i * nb + b, 0)),
                  pl.BlockSpec((2, BLK, LANES), lambda i, b: (_SLAB["ka"] // 2, i * nb + b, 0))],
        out_specs=[pl.BlockSpec((q_slabs, BLK, LANES), lambda i, b: (0, i * nb + b, 0)),
                   pl.BlockSpec((1, BLK, 2 * A_KV_WIDTH), lambda i, b: (i, 0, 0))],
        out_shape=[jax.ShapeDtypeStruct((q_slabs, n * l, LANES), jnp.float32),
                   jax.ShapeDtypeStruct((n, BLK, 2 * A_KV_WIDTH), jnp.float32)],
        scratch_shapes=[pltpu.VMEM((2 * BLK, LANES), jnp.bfloat16),
                        pltpu.VMEM((2 * BLK, LANES), jnp.bfloat16),
                        pltpu.VMEM((A_Q_HEADS * BLK, 2 * BLK), jnp.float32),
                        pltpu.VMEM((A_Q_HEADS * BLK, 2 * BLK), jnp.bfloat16)],
        compiler_params=_params(2),
        name="attn_a",
    )(sink, h3, h3)


def _attn_b_kernel(q_ref, k_ref, v_ref, o_ref, lse_ref, st_ref, kv_scr, s_scr, p_scr, *,
                   n_blocks, dil):
    sb = pl.program_id(1)
    n_col = B_KV_WIDTH // LANES

    @pl.when(sb == 0)
    def _():
        kv_scr[:, :, BLK:2 * BLK, :] = jnp.zeros((dil, 2, BLK, B_KV_WIDTH), jnp.bfloat16)

    @pl.when(sb == n_blocks - 1)
    def _():
        for c in range(n_col):
            st_ref[0, :, c * LANES:(c + 1) * LANES] = k_ref[c]
            st_ref[0, :, B_KV_WIDTH + c * LANES:B_KV_WIDTH + (c + 1) * LANES] = v_ref[c]

    lo = lax.broadcasted_iota(jnp.int32, (BLK, LANES), 1) < HALF
    zeros = jnp.zeros((BLK, LANES), jnp.bfloat16)
    mask = _band_mask(sb, 0, BLK)
    group = B_Q_HEADS // B_KV_HEADS

    def residue(r):
        rows = pl.ds(r, BLK, stride=dil) if dil > 1 else slice(None)
        for c in range(n_col):
            kv_scr[r, 0, _cur_half(sb), c * LANES:(c + 1) * LANES] = (
                k_ref[c, rows, :].astype(jnp.bfloat16))
            kv_scr[r, 1, _cur_half(sb), c * LANES:(c + 1) * LANES] = (
                v_ref[c, rows, :].astype(jnp.bfloat16))
        pieces = []
        for g in range(B_KV_HEADS):
            slab = q_ref[g, rows, :] * SCALE
            for sub in range(group):
                cols = [zeros] * n_col
                cols[g // 2] = _to_half(slab, sub, g % 2, lo).astype(jnp.bfloat16)
                pieces.append(jnp.concatenate(cols, axis=1))
        qz = jnp.concatenate(pieces, axis=0)
        s_scr[...] = _nt(qz, kv_scr[r, 0])
        inv, lse = [], []
        for h in range(B_Q_HEADS):
            s = jnp.where(mask, s_scr[h * BLK:(h + 1) * BLK, :], -jnp.inf)
            m = jnp.max(s, axis=1, keepdims=True)
            p = jnp.exp(s - m)
            den = jnp.sum(p, axis=1, keepdims=True)
            p_scr[h * BLK:(h + 1) * BLK, :] = p.astype(jnp.bfloat16)
            inv.append(1.0 / den)
            lse.append(m + jnp.log(den))
        o_all = jnp.dot(p_scr[...], kv_scr[r, 1], preferred_element_type=jnp.float32)
        for g in range(B_KV_HEADS):
            col = slice((g // 2) * LANES, (g // 2 + 1) * LANES)
            e = o_all[(2 * g) * BLK:(2 * g + 1) * BLK, col] * inv[2 * g]
            o = o_all[(2 * g + 1) * BLK:(2 * g + 2) * BLK, col] * inv[2 * g + 1]
            o_ref[g, rows, :] = _merge_halves(e, o, g % 2, lo)
            lse_ref[g, rows, :] = jnp.where(lo, jnp.broadcast_to(lse[2 * g], (BLK, LANES)),
                                            jnp.broadcast_to(lse[2 * g + 1], (BLK, LANES)))

    if dil <= 4:
        for r in range(dil):
            residue(r)
    else:
        lax.fori_loop(0, dil, lambda r, _: residue(r), None)


def _attn_b(h3, gi, dil, n, l):
    rows = BLK * dil
    nsb = l // rows
    q_slabs = B_WIDTH // LANES
    kv_slabs = B_KV_WIDTH // LANES
    tok = lambda i, sb: i * nsb + sb
    out_blk = pl.BlockSpec((q_slabs, rows, LANES), lambda i, sb: (0, tok(i, sb), 0))
    o, lse, st = pl.pallas_call(
        functools.partial(_attn_b_kernel, n_blocks=nsb, dil=dil),
        grid=(n, nsb),
        in_specs=[pl.BlockSpec((q_slabs, rows, LANES),
                               lambda i, sb: (_SLAB[f"qb{gi}"] // q_slabs, tok(i, sb), 0)),
                  pl.BlockSpec((kv_slabs, rows, LANES),
                               lambda i, sb: (_SLAB[f"kb{gi}"] // kv_slabs, tok(i, sb), 0)),
                  pl.BlockSpec((kv_slabs, rows, LANES),
                               lambda i, sb: (_SLAB[f"vb{gi}"] // kv_slabs, tok(i, sb), 0))],
        out_specs=[out_blk, out_blk,
                   pl.BlockSpec((1, rows, 2 * B_KV_WIDTH), lambda i, sb: (i, 0, 0))],
        out_shape=[jax.ShapeDtypeStruct((q_slabs, n * l, LANES), jnp.float32),
                   jax.ShapeDtypeStruct((q_slabs, n * l, LANES), jnp.float32),
                   jax.ShapeDtypeStruct((n, rows, 2 * B_KV_WIDTH), jnp.float32)],
        scratch_shapes=[pltpu.VMEM((dil, 2, 2 * BLK, B_KV_WIDTH), jnp.bfloat16),
                        pltpu.VMEM((B_Q_HEADS * BLK, 2 * BLK), jnp.float32),
                        pltpu.VMEM((B_Q_HEADS * BLK, 2 * BLK), jnp.bfloat16)],
        compiler_params=_params(2),
        name=f"attn_b{gi}",
    )(h3, h3, h3)
    return o, lse, st


_PAD = 16


def _pad_rows(x, rows=_PAD, at=0):
    parts = []
    if at:
        parts.append(jnp.zeros((at, x.shape[1]), x.dtype))
    parts.append(x)
    if rows - at - x.shape[0]:
        parts.append(jnp.zeros((rows - at - x.shape[0], x.shape[1]), x.dtype))
    return jnp.concatenate(parts, axis=0)


def _sample_attend(qz, cache_ref, k_new, v_new, width, dil, max_dist, sink=None):
    cb = cache_ref.shape[2]
    qz = qz.astype(jnp.bfloat16)
    s_c = jnp.dot(qz, cache_ref[0, 0:width, :].astype(jnp.bfloat16),
                  preferred_element_type=jnp.float32)
    s_n = _nt(qz, _pad_rows(k_new).astype(jnp.bfloat16))
    parts = []
    for s, base in ((s_c, 0), (s_n, cb)):
        t = lax.broadcasted_iota(jnp.int32, s.shape, 0) & (DEC_SEQ - 1)
        r = lax.broadcasted_iota(jnp.int32, s.shape, 1) + base
        dist = cb + t - r
        valid = (dist >= 0) & (dist <= max_dist) & ((dist & (dil - 1)) == 0)
        parts.append(jnp.where(valid, s, -jnp.inf))
    s_c, s_n = parts
    m = jnp.maximum(jnp.max(s_c, axis=1, keepdims=True), jnp.max(s_n, axis=1, keepdims=True))
    if sink is not None:
        m = jnp.maximum(m, sink)
    p_c, p_n = jnp.exp(s_c - m), jnp.exp(s_n - m)
    den = jnp.sum(p_c, axis=1, keepdims=True) + jnp.sum(p_n, axis=1, keepdims=True)
    if sink is not None:
        den = den + jnp.exp(sink - m)
    o = (_nt(p_c.astype(jnp.bfloat16), cache_ref[0, width:2 * width, :].astype(jnp.bfloat16))
         + jnp.dot(p_n.astype(jnp.bfloat16), _pad_rows(v_new).astype(jnp.bfloat16),
                   preferred_element_type=jnp.float32))
    return o / den, m + jnp.log(den)


def _shift_cache(cache_ref, new_ref, new_rows):
    cb = cache_ref.shape[2]
    lane = lax.broadcasted_iota(jnp.int32, (cache_ref.shape[1], LANES), 1)
    keep = lane < LANES - DEC_SEQ
    tail = jnp.transpose(_pad_rows(new_rows, LANES, LANES - DEC_SEQ))
    cur = pltpu.roll(cache_ref[0, :, 0:LANES], LANES - DEC_SEQ, 1)
    for j in range(cb // LANES):
        if j + 1 < cb // LANES:
            nxt = pltpu.roll(cache_ref[0, :, (j + 1) * LANES:(j + 2) * LANES], LANES - DEC_SEQ, 1)
        else:
            nxt = tail
        new_ref[0, :, j * LANES:(j + 1) * LANES] = jnp.where(keep, cur, nxt)
        cur = nxt


def _sample_kernel(h_ref, ca_ref, c0_ref, c1_ref, c2_ref, sink_ref,
                   oa_ref, ob_ref, na_ref, n0_ref, n1_ref, n2_ref):
    lo = lax.broadcasted_iota(jnp.int32, (DEC_SEQ, LANES), 1) < HALF
    slabs = lambda name, count: [h_ref[_SLAB[name] + c] for c in range(count)]

    group = A_Q_HEADS // A_KV_HEADS
    pieces = []
    for pair, slab in enumerate(slabs("qa", A_WIDTH // LANES)):
        for sub in range(2):
            pieces.append(_to_half(slab * SCALE, sub, (2 * pair + sub) // group, lo))
    k_new, v_new = h_ref[_SLAB["ka"]], h_ref[_SLAB["va"]]
    o, _ = _sample_attend(jnp.concatenate(pieces, axis=0), ca_ref, k_new, v_new,
                          A_KV_WIDTH, 1, A_WINDOW - 1, sink_ref[...])
    for pair in range(A_Q_HEADS // 2):
        oa_ref[pair] = _merge_halves(o[(2 * pair) * DEC_SEQ:(2 * pair + 1) * DEC_SEQ, :],
                                     o[(2 * pair + 1) * DEC_SEQ:(2 * pair + 2) * DEC_SEQ, :],
                                     (2 * pair) // group, lo)
    _shift_cache(ca_ref, na_ref, jnp.concatenate([k_new, v_new], axis=1))

    n_col = B_KV_WIDTH // LANES
    zeros = jnp.zeros((DEC_SEQ, LANES), jnp.float32)
    outs, lses = [], []
    for gi, ((win, dil), c_ref, n_ref) in enumerate(
            zip(B_PAIRS, (c0_ref, c1_ref, c2_ref), (n0_ref, n1_ref, n2_ref))):
        pieces = []
        for g, slab in enumerate(slabs(f"qb{gi}", B_WIDTH // LANES)):
            for sub in range(2):
                cols = [zeros] * n_col
                cols[g // 2] = _to_half(slab * SCALE, sub, g % 2, lo)
                pieces.append(jnp.concatenate(cols, axis=1))
        k_new = jnp.concatenate(slabs(f"kb{gi}", n_col), axis=1)
        v_new = jnp.concatenate(slabs(f"vb{gi}", n_col), axis=1)
        o, lse = _sample_attend(jnp.concatenate(pieces, axis=0), c_ref, k_new, v_new,
                                B_KV_WIDTH, dil, win)
        outs.append(o), lses.append(lse)
        _shift_cache(c_ref, n_ref, jnp.concatenate([k_new, v_new], axis=1))
    top = functools.reduce(jnp.maximum, lses)
    es = [jnp.exp(v - top) for v in lses]
    tot = functools.reduce(lambda a, c: a + c, es)
    o = functools.reduce(lambda a, c: a + c, [(e / tot) * v for e, v in zip(es, outs)])
    for g in range(B_KV_HEADS):
        col = slice((g // 2) * LANES, (g // 2 + 1) * LANES)
        ob_ref[g] = _merge_halves(o[(2 * g) * DEC_SEQ:(2 * g + 1) * DEC_SEQ, col],
                                  o[(2 * g + 1) * DEC_SEQ:(2 * g + 2) * DEC_SEQ, col], g % 2, lo)


def _sample_step(hs3, caches_t, sink_rows):
    n = caches_t[0].shape[0]
    cache_specs = [pl.BlockSpec((1,) + c.shape[1:], lambda i: (i, 0, 0)) for c in caches_t]
    row_blk = lambda slabs: pl.BlockSpec((slabs, DEC_SEQ, LANES), lambda i: (0, i, 0))
    return pl.pallas_call(
        _sample_kernel,
        grid=(n,),
        in_specs=[row_blk(N_SLABS)] + cache_specs
        + [pl.BlockSpec((A_Q_HEADS * DEC_SEQ, 1), lambda i: (0, 0))],
        out_specs=[row_blk(A_WIDTH // LANES), row_blk(B_WIDTH // LANES)] + cache_specs,
        out_shape=[jax.ShapeDtypeStruct((A_WIDTH // LANES, n * DEC_SEQ, LANES), jnp.float32),
                   jax.ShapeDtypeStruct((B_WIDTH // LANES, n * DEC_SEQ, LANES), jnp.float32)]
        + [jax.ShapeDtypeStruct(c.shape, jnp.float32) for c in caches_t],
        compiler_params=_params(1),
        name="sample_step",
    )(hs3, *caches_t, sink_rows)


def _sigmoid(x):
    return 1.0 / (1.0 + jnp.exp(-x))


def _out_kernel(*refs, n_groups):
    x_ref, oa_ref, za_ref, ga_ref, gb_ref, zb_ref = refs[:6]
    ob_refs = refs[6:6 + n_groups]
    lse_refs = refs[6 + n_groups:6 + 2 * n_groups] if n_groups > 1 else ()
    wa_ref, wb_ref, wo_ref, lng_ref, lnb_ref, y_ref = refs[-6:]

    def gated(o, z):
        return (o * (z * _sigmoid(z))).astype(jnp.bfloat16)

    ya = jnp.concatenate([gated(oa_ref[s], za_ref[s]) for s in range(A_WIDTH // LANES)], axis=1)
    yb = []
    for s in range(B_WIDTH // LANES):
        if n_groups > 1:
            lses = [r[s] for r in lse_refs]
            top = functools.reduce(jnp.maximum, lses)
            es = [jnp.exp(v - top) for v in lses]
            tot = functools.reduce(lambda a, c: a + c, es)
            ob = functools.reduce(lambda a, c: a + c,
                                  [(e / tot) * r[s] for e, r in zip(es, ob_refs)])
        else:
            ob = ob_refs[0][s]
        yb.append(gated(ob, zb_ref[s]))
    yb = jnp.concatenate(yb, axis=1)
    da = jnp.dot(ya, wa_ref[...], preferred_element_type=jnp.float32)
    db = jnp.dot(yb, wb_ref[...], preferred_element_type=jnp.float32)
    m = jnp.concatenate(
        [(_sigmoid(ga_ref[s]) * da[:, s * LANES:(s + 1) * LANES]
          + _sigmoid(gb_ref[s]) * db[:, s * LANES:(s + 1) * LANES]).astype(jnp.bfloat16)
         for s in range(D_MODEL // LANES)], axis=1)
    h = DN_ALPHA * x_ref[...] + jnp.dot(m, wo_ref[...], preferred_element_type=jnp.float32)
    mu = jnp.mean(h, axis=-1, keepdims=True)
    var = jnp.mean(jnp.square(h - mu), axis=-1, keepdims=True)
    y_ref[...] = (h - mu) * lax.rsqrt(var + LN_EPS) * lng_ref[...] + lnb_ref[...]


def _out_proj(x, h3, oa, obs, lses, wa, wb, wo, ln_g, ln_b, tm=256):
    m = x.shape[0]
    n_groups = len(obs)

    def slab_blk(name_or_none, count):
        first = 0 if name_or_none is None else _SLAB[name_or_none] // count
        return pl.BlockSpec((count, tm, LANES), lambda i: (first, i, 0))

    row = pl.BlockSpec((tm, D_MODEL), lambda i: (i, 0))
    whole = lambda a: pl.BlockSpec(a.shape, lambda i: (0, 0), pipeline_mode=pl.Buffered(1))
    a_slabs, b_slabs, d_slabs = A_WIDTH // LANES, B_WIDTH // LANES, D_MODEL // LANES
    in_specs = ([row, slab_blk(None, a_slabs), slab_blk("za", a_slabs), slab_blk("ga", d_slabs),
                 slab_blk("gb", d_slabs), slab_blk("zb", b_slabs)]
                + [slab_blk(None, b_slabs)] * (len(obs) + len(lses))
                + [whole(wa), whole(wb), whole(wo), whole(ln_g), whole(ln_b)])
    return pl.pallas_call(
        functools.partial(_out_kernel, n_groups=n_groups),
        grid=(m // tm,),
        in_specs=in_specs,
        out_specs=row,
        out_shape=jax.ShapeDtypeStruct((m, D_MODEL), jnp.float32),
        compiler_params=_params(1),
        name=f"out_proj{n_groups}",
    )(x, oa, h3, h3, h3, h3, *obs, *lses, wa, wb, wo, ln_g, ln_b)


def _permute_cols(a):
    return jnp.concatenate([a[..., _ORIG[s][0]:_ORIG[s][0] + _ORIG[s][1]] for s in _SEGS], axis=-1)


def kernel(x_prompt, x_sample, cache_a_kv, cache_b0_kv, cache_b1_kv, cache_b2_kv,
           w_in, b_in, sink_a, w_br_a, w_br_b, w_out, ln_g, ln_b):
    n, l, _ = x_prompt.shape
    ns, t, _ = x_sample.shape
    assert w_in.shape[0] == 1 and t == DEC_SEQ and l == SEQ
    assert PAST_LEN >= max(w for w, _ in B_PAIRS)
    w_bf = _permute_cols(w_in[0]).astype(jnp.bfloat16)
    b_p = _permute_cols(b_in[0])[None, :]
    wa, wb, wo = (w[0].astype(jnp.bfloat16) for w in (w_br_a, w_br_b, w_out))
    tabs_p = _rope_tables(jnp.arange(l, dtype=jnp.float32))
    tabs_s = [jnp.tile(tb, (ns, 1)) for tb in
              _rope_tables(PAST_LEN + jnp.arange(t, dtype=jnp.float32))]
    sink = sink_a[0]
    sink_rows = jnp.repeat(sink, DEC_SEQ)[:, None]

    xp = x_prompt.reshape(n * l, D_MODEL)
    hp = _in_proj(xp, w_bf, b_p, tabs_p)
    oa, st_a = _attn_a(hp, sink, n, l)
    obs, lses, st_b = [], [], []
    for gi, (_, dil) in enumerate(B_PAIRS):
        o, lse, st = _attn_b(hp, gi, dil, n, l)
        obs.append(o), lses.append(lse), st_b.append(st)
    yp = _out_proj(xp, hp, oa, obs, lses, wa, wb, wo, ln_g, ln_b)

    xs = x_sample.reshape(ns * t, D_MODEL)
    hs = _in_proj(xs, w_bf, b_p, tabs_s)
    to_t = lambda c, w: jnp.transpose(c[0].reshape(ns, -1, w), (0, 2, 1))
    caches_t = [to_t(cache_a_kv, 2 * A_KV_WIDTH)] + [
        to_t(c, 2 * B_KV_WIDTH) for c in (cache_b0_kv, cache_b1_kv, cache_b2_kv)]
    oa_s, ob_s, *new_t = _sample_step(hs, caches_t, sink_rows)
    ys = _out_proj(xs, hs, oa_s, [ob_s], [], wa, wb, wo, ln_g, ln_b)

    kv_shape = lambda a, heads: a.reshape((1, a.shape[0], a.shape[1], 2, heads, HEAD_DIM))
    from_t = lambda a, heads: kv_shape(jnp.transpose(a, (0, 2, 1)), heads)
    return (yp.reshape(n, l, D_MODEL), ys.reshape(ns, t, D_MODEL),
            kv_shape(st_a, A_KV_HEADS), *[kv_shape(s, B_KV_HEADS) for s in st_b],
            from_t(new_t[0], A_KV_HEADS), *[from_t(c, B_KV_HEADS) for c in new_t[1:]])
```

```python
import functools

import jax
import jax.numpy as jnp
import numpy as np
from jax import lax
from jax.experimental import pallas as pl
from jax.experimental.pallas import tpu as pltpu

D_MODEL = 2048
SEQ = 8192
DEC_SEQ = 8
PAST_LEN = 16384
HEAD_DIM = 64
ROT_DIM = HEAD_DIM // 4
ROPE_THETA = 500000.0
BLK = 128
A_WINDOW = 128
A_Q_HEADS = 16
A_KV_HEADS = 2
B_PAIRS = ((128, 1), (512, 4), (2048, 16))
B_Q_HEADS = 8
B_KV_HEADS = 4
A_WIDTH = A_Q_HEADS * HEAD_DIM
B_WIDTH = B_Q_HEADS * HEAD_DIM
A_KV_WIDTH = A_KV_HEADS * HEAD_DIM
B_KV_WIDTH = B_KV_HEADS * HEAD_DIM
SCALE = HEAD_DIM ** -0.5
DN_ALPHA = 2.0 ** 0.25
LN_EPS = 1e-5
LANES = 128
HALF = LANES // 2

_ORIG = {}
_off = 0
for _name, _w in ([("qa", A_WIDTH), ("ka", A_KV_WIDTH), ("va", A_KV_WIDTH), ("za", A_WIDTH)]
                  + [(f"{p}b{i}", w) for i in range(3)
                     for p, w in (("q", B_WIDTH), ("k", B_KV_WIDTH), ("v", B_KV_WIDTH))]
                  + [("zb", B_WIDTH), ("ga", D_MODEL), ("gb", D_MODEL)]):
    _ORIG[_name] = (_off, _w)
    _off += _w
IN_WIDTH = _off
N_SLABS = IN_WIDTH // LANES

_SEGS = ["qa", "qb0", "qb1", "qb2", "kb0", "kb1", "kb2", "ka", "va",
         "vb0", "vb1", "ga", "gb", "za", "zb", "vb2"]
_SLAB = {}
_off = 0
for _name in _SEGS:
    _SLAB[_name] = _off // LANES
    assert _SLAB[_name] % min(_ORIG[_name][1] // LANES, 16) == 0 or _name in ("ka", "va")
    _off += _ORIG[_name][1]
assert _off == IN_WIDTH and _SLAB["va"] == _SLAB["ka"] + 1 and _SLAB["ka"] % 2 == 0
N_ROPE_SLABS = _SLAB["va"]

_VMEM_LIMIT = 56 * 1024 * 1024


def _params(n_axes, vmem=_VMEM_LIMIT):
    return pltpu.CompilerParams(dimension_semantics=("arbitrary",) * n_axes,
                                vmem_limit_bytes=vmem)


def _rope_tables(pos):
    half = ROT_DIM // 2
    inv = ROPE_THETA ** (-jnp.arange(0, ROT_DIM, 2, dtype=jnp.float32) / ROT_DIM)
    dim = np.arange(LANES) % HEAD_DIM
    ang = pos[:, None] * inv[dim % half][None, :]
    cos, sin = jnp.cos(ang), jnp.sin(ang)
    first, second = (dim < half)[None, :], ((dim >= half) & (dim < ROT_DIM))[None, :]
    c = jnp.where(first | second, cos, 1.0)
    s1 = jnp.where(first, -sin, 0.0)
    s2 = jnp.where(second, sin, 0.0)
    return c, s1, s2


def _rope(h, c, s1, s2):
    half = ROT_DIM // 2
    return h * c + pltpu.roll(h, LANES - half, 1) * s1 + pltpu.roll(h, half, 1) * s2


def _to_half(piece, src_half, dst_half, lo):
    if src_half != dst_half:
        piece = pltpu.roll(piece, HALF, 1)
    keep = lo if dst_half == 0 else jnp.logical_not(lo)
    return jnp.where(keep, piece, 0.0)


def _merge_halves(even, odd, src_half, lo):
    if src_half == 1:
        even = pltpu.roll(even, HALF, 1)
    else:
        odd = pltpu.roll(odd, HALF, 1)
    return jnp.where(lo, even, odd)


def _nt(a, b):
    return lax.dot_general(a, b, (((1,), (1,)), ((), ())), preferred_element_type=jnp.float32)


def _band_bias(lo_off, hi_off, no_previous):
    i = lax.broadcasted_iota(jnp.int32, (BLK, 2 * BLK), 0)
    j = lax.broadcasted_iota(jnp.int32, (BLK, 2 * BLK), 1)
    band = (j >= i + lo_off) & (j <= i + hi_off)
    first_key = jnp.where(no_previous, BLK, 0)
    return (jnp.where(band, 0.0, -jnp.inf), jnp.where(band & (j >= first_key), 0.0, -jnp.inf))


def _softmax_pv(s, bias, v_win, sink=None):
    s = s + bias
    m = jnp.max(s, axis=1, keepdims=True)
    if sink is not None:
        m = jnp.maximum(m, sink)
    p = jnp.exp(s - m)
    den = jnp.sum(p, axis=1, keepdims=True)
    if sink is not None:
        den = den + jnp.exp(sink - m)
    o = jnp.dot(p.astype(jnp.bfloat16), v_win, preferred_element_type=jnp.float32)
    return o * (1.0 / den), m + jnp.log(den)


def _in_proj_kernel(x_ref, w_ref, b_ref, c_ref, s1_ref, s2_ref, o_ref, xb_ref, *, slabs):
    j = pl.program_id(1)

    @pl.when(j == 0)
    def _():
        xb_ref[...] = x_ref[...].astype(jnp.bfloat16)

    def tile(rope_slab):
        for c in range(slabs // 2):
            cols = slice(c * 2 * LANES, (c + 1) * 2 * LANES)
            acc = jnp.dot(xb_ref[...], w_ref[:, cols],
                          preferred_element_type=jnp.float32) + b_ref[:, cols]
            for t in range(2):
                col = acc[:, t * LANES:(t + 1) * LANES]
                if rope_slab[2 * c + t]:
                    col = _rope(col, c_ref[...], s1_ref[...], s2_ref[...])
                o_ref[2 * c + t] = col

    full, rest = divmod(N_ROPE_SLABS, slabs)
    pl.when(j < full)(lambda: tile([True] * slabs))
    if rest:
        pl.when(j == full)(lambda: tile([s < rest for s in range(slabs)]))
    pl.when(j >= full + (1 if rest else 0))(lambda: tile([False] * slabs))


def _in_proj(x, w_bf, b, tabs, tm=1024, tn=768):
    m, k = x.shape
    n = w_bf.shape[1]
    n_pos_blocks = tabs[0].shape[0] // tm
    tab = pl.BlockSpec((tm, LANES), lambda i, j: (i % n_pos_blocks, 0))
    return pl.pallas_call(
        functools.partial(_in_proj_kernel, slabs=tn // LANES),
        grid=(m // tm, n // tn),
        in_specs=[pl.BlockSpec((tm, k), lambda i, j: (i, 0)),
                  pl.BlockSpec((k, tn), lambda i, j: (0, j)),
                  pl.BlockSpec((1, tn), lambda i, j: (0, j)), tab, tab, tab],
        out_specs=pl.BlockSpec((tn // LANES, tm, LANES), lambda i, j: (j, i, 0)),
        out_shape=jax.ShapeDtypeStruct((n // LANES, m, LANES), jnp.float32),
        scratch_shapes=[pltpu.VMEM((tm, k), jnp.bfloat16)],
        compiler_params=_params(2),
        name="in_proj",
    )(x, w_bf, b, *tabs)


_UNITS_PER_TRIP = 2


def _attend_pairs(slabs, k_wins, v_wins, kv_of_pair, bias, lo, sinks=None):
    result = []
    for p, slab in enumerate(slabs):
        col, half = kv_of_pair[p]
        slab = slab * SCALE
        outs, lses = [], []
        for sub in range(2):
            q_h = _to_half(slab, sub, half, lo).astype(jnp.bfloat16)
            o, lse = _softmax_pv(_nt(q_h, k_wins[col]), bias, v_wins[col],
                                 None if sinks is None else sinks[2 * p + sub])
            outs.append(o), lses.append(lse)
        result.append((_merge_halves(outs[0], outs[1], half, lo),
                       jnp.where(lo, jnp.broadcast_to(lses[0], (BLK, LANES)),
                                 jnp.broadcast_to(lses[1], (BLK, LANES)))))
    return result


def _attn_a_kernel(sink_ref, q_ref, kv_ref, kvp_ref, o_ref, k_buf, v_buf, *, n_sub):
    lo = lax.broadcasted_iota(jnp.int32, (BLK, LANES), 1) < HALF
    seq_start = pl.program_id(1) == 0
    group = A_Q_HEADS // A_KV_HEADS
    for buf, c in ((k_buf, 0), (v_buf, 1)):
        buf[0:BLK, :] = kvp_ref[c].astype(jnp.bfloat16)
        buf[BLK:, :] = kv_ref[c].astype(jnp.bfloat16)
    sinks = [sink_ref[h] for h in range(A_Q_HEADS)]
    kv_of_pair = [(0, (2 * p) // group) for p in range(A_Q_HEADS // 2)]

    def block(j, _):
        start = pl.multiple_of(j * BLK, BLK)
        rows = pl.ds(start, BLK)
        window = pl.ds(start, 2 * BLK)
        _, bias = _band_bias(1, A_WINDOW, seq_start & (j == 0))
        outs = _attend_pairs([q_ref[p, rows, :] for p in range(A_Q_HEADS // 2)],
                             [k_buf[window, :]], [v_buf[window, :]], kv_of_pair, bias, lo, sinks)
        for p, (o, _) in enumerate(outs):
            o_ref[p, rows, :] = o

    lax.fori_loop(0, n_sub, block, None)


def _attn_a(h3, sink, n, l, n_sub=4):
    rows = BLK * n_sub
    nsb = l // rows
    q_slabs = A_WIDTH // LANES
    tok = lambda i, sb: i * nsb + sb
    prev_blk = lambda i, sb: jnp.maximum(tok(i, sb) * n_sub - 1, 0)
    return pl.pallas_call(
        functools.partial(_attn_a_kernel, n_sub=n_sub),
        grid=(n, nsb),
        in_specs=[pl.BlockSpec(memory_space=pltpu.SMEM),
                  pl.BlockSpec((q_slabs, rows, LANES),
                               lambda i, sb: (_SLAB["qa"] // q_slabs, tok(i, sb), 0)),
                  pl.BlockSpec((2, rows, LANES), lambda i, sb: (_SLAB["ka"] // 2, tok(i, sb), 0)),
                  pl.BlockSpec((2, BLK, LANES),
                               lambda i, sb: (_SLAB["ka"] // 2, prev_blk(i, sb), 0))],
        out_specs=pl.BlockSpec((q_slabs, rows, LANES), lambda i, sb: (0, tok(i, sb), 0)),
        out_shape=jax.ShapeDtypeStruct((q_slabs, n * l, LANES), jnp.float32),
        scratch_shapes=[pltpu.VMEM((rows + BLK, LANES), jnp.bfloat16)] * 2,
        compiler_params=_params(2),
        name="attn_a",
    )(sink, h3, h3, h3)


def _attn_b_kernel(q_ref, k_ref, v_ref, kp_ref, vp_ref, o_ref, lse_ref, k_buf, v_buf, *,
                   dil, n_sub):
    n_col = B_KV_WIDTH // LANES
    lo = lax.broadcasted_iota(jnp.int32, (BLK, LANES), 1) < HALF
    seq_start = pl.program_id(1) == 0
    kv_of_pair = [(g // 2, g % 2) for g in range(B_KV_HEADS)]

    def fold(r, j):
        if dil == 1:
            return pl.ds(pl.multiple_of(j * BLK, BLK), BLK)
        return pl.ds(j * (BLK * dil) + r, BLK, stride=dil)

    def fill(r, _):
        for buf, prev, cur in ((k_buf, kp_ref, k_ref), (v_buf, vp_ref, v_ref)):
            for c in range(n_col):
                buf[r, c, 0:BLK, :] = prev[c, fold(r, 0), :].astype(jnp.bfloat16)
                for j in range(n_sub):
                    buf[r, c, (j + 1) * BLK:(j + 2) * BLK, :] = (
                        cur[c, fold(r, j), :].astype(jnp.bfloat16))

    def block(u, _):
        r, j = u // n_sub, u % n_sub
        window = pl.ds(pl.multiple_of(j * BLK, BLK), 2 * BLK)
        _, bias = _band_bias(0, BLK, seq_start & (j == 0))
        outs = _attend_pairs([q_ref[g, fold(r, j), :] for g in range(B_KV_HEADS)],
                             [k_buf[r, c, window, :] for c in range(n_col)],
                             [v_buf[r, c, window, :] for c in range(n_col)], kv_of_pair, bias, lo)
        for g, (o, lse) in enumerate(outs):
            o_ref[g, fold(r, j), :] = o
            lse_ref[g, fold(r, j), :] = lse

    if dil == 1:
        fill(0, None)
    else:
        lax.fori_loop(0, dil, fill, None)
    lax.fori_loop(0, dil * n_sub, block, None, unroll=_UNITS_PER_TRIP)


def _attn_b(h3, gi, dil, n, l, n_sub):
    prev_rows = BLK * dil
    rows = prev_rows * n_sub
    nsb = l // rows
    q_slabs = B_WIDTH // LANES
    kv_slabs = B_KV_WIDTH // LANES
    tok = lambda i, sb: i * nsb + sb
    prev_blk = lambda i, sb: jnp.maximum(tok(i, sb) * n_sub - 1, 0)
    out_blk = pl.BlockSpec((q_slabs, rows, LANES), lambda i, sb: (0, tok(i, sb), 0))

    prev_mode = dict(pipeline_mode=pl.Buffered(1)) if dil > 4 else {}

    def kv_specs(name):
        first = _SLAB[f"{name}{gi}"] // kv_slabs
        return (pl.BlockSpec((kv_slabs, rows, LANES), lambda i, sb: (first, tok(i, sb), 0)),
                pl.BlockSpec((kv_slabs, prev_rows, LANES),
                             lambda i, sb: (first, prev_blk(i, sb), 0), **prev_mode))

    (k_cur, k_prev), (v_cur, v_prev) = kv_specs("kb"), kv_specs("vb")
    return pl.pallas_call(
        functools.partial(_attn_b_kernel, dil=dil, n_sub=n_sub),
        grid=(n, nsb),
        in_specs=[pl.BlockSpec((q_slabs, rows, LANES),
                               lambda i, sb: (_SLAB[f"qb{gi}"] // q_slabs, tok(i, sb), 0)),
                  k_cur, v_cur, k_prev, v_prev],
        out_specs=[out_blk, out_blk],
        out_shape=[jax.ShapeDtypeStruct((q_slabs, n * l, LANES), jnp.float32)] * 2,
        scratch_shapes=[pltpu.VMEM((dil, kv_slabs, (n_sub + 1) * BLK, LANES), jnp.bfloat16)] * 2,
        compiler_params=_params(2),
        name=f"attn_b{gi}",
    )(h3, h3, h3, h3, h3)


def _kv_state_kernel(k_ref, v_ref, o_ref):
    n_k = k_ref.shape[0]
    for c in range(n_k):
        o_ref[0, c * LANES:(c + 1) * LANES, :] = jnp.transpose(k_ref[c])
        o_ref[0, (n_k + c) * LANES:(n_k + c + 1) * LANES, :] = jnp.transpose(v_ref[c])


def _kv_state(h3, k_name, v_name, width, win, n, l, name):
    slabs = width // LANES
    chunk = min(win, 4 * BLK)
    per_seq = win // chunk
    first_chunk = lambda i, t: ((i + 1) * l - win) // chunk + t
    spec = lambda s: pl.BlockSpec((slabs, chunk, LANES),
                                  lambda i, t: (_SLAB[s] // slabs, first_chunk(i, t), 0))
    return pl.pallas_call(
        _kv_state_kernel,
        grid=(n, per_seq),
        in_specs=[spec(k_name), spec(v_name)],
        out_specs=pl.BlockSpec((1, 2 * width, chunk), lambda i, t: (i, 0, t)),
        out_shape=jax.ShapeDtypeStruct((n, 2 * width, win), jnp.float32),
        compiler_params=_params(2),
        name=name,
    )(h3, h3)


_PAD = 16


def _pad_rows(x, rows=_PAD, at=0):
    parts = []
    if at:
        parts.append(jnp.zeros((at, x.shape[1]), x.dtype))
    parts.append(x)
    if rows - at - x.shape[0]:
        parts.append(jnp.zeros((rows - at - x.shape[0], x.shape[1]), x.dtype))
    return jnp.concatenate(parts, axis=0)


def _sample_attend(qz, cache_ref, k_new, v_new, width, dil, max_dist, sink=None):
    cb = cache_ref.shape[2]
    qz = qz.astype(jnp.bfloat16)
    s_c = jnp.dot(qz, cache_ref[0, 0:width, :].astype(jnp.bfloat16),
                  preferred_element_type=jnp.float32)
    s_n = _nt(qz, _pad_rows(k_new).astype(jnp.bfloat16))
    parts = []
    for s, base in ((s_c, 0), (s_n, cb)):
        t = lax.broadcasted_iota(jnp.int32, s.shape, 0) & (DEC_SEQ - 1)
        r = lax.broadcasted_iota(jnp.int32, s.shape, 1) + base
        dist = cb + t - r
        valid = (dist >= 0) & (dist <= max_dist) & ((dist & (dil - 1)) == 0)
        parts.append(jnp.where(valid, s, -jnp.inf))
    s_c, s_n = parts
    m = jnp.maximum(jnp.max(s_c, axis=1, keepdims=True), jnp.max(s_n, axis=1, keepdims=True))
    if sink is not None:
        m = jnp.maximum(m, sink)
    p_c, p_n = jnp.exp(s_c - m), jnp.exp(s_n - m)
    den = jnp.sum(p_c, axis=1, keepdims=True) + jnp.sum(p_n, axis=1, keepdims=True)
    if sink is not None:
        den = den + jnp.exp(sink - m)
    o = (_nt(p_c.astype(jnp.bfloat16), cache_ref[0, width:2 * width, :].astype(jnp.bfloat16))
         + jnp.dot(p_n.astype(jnp.bfloat16), _pad_rows(v_new).astype(jnp.bfloat16),
                   preferred_element_type=jnp.float32))
    return o / den, m + jnp.log(den)


def _shift_cache(cache_ref, new_ref, new_rows):
    cb = cache_ref.shape[2]
    lane = lax.broadcasted_iota(jnp.int32, (cache_ref.shape[1], LANES), 1)
    keep = lane < LANES - DEC_SEQ
    tail = jnp.transpose(_pad_rows(new_rows, LANES, LANES - DEC_SEQ))
    cur = pltpu.roll(cache_ref[0, :, 0:LANES], LANES - DEC_SEQ, 1)
    for j in range(cb // LANES):
        if j + 1 < cb // LANES:
            nxt = pltpu.roll(cache_ref[0, :, (j + 1) * LANES:(j + 2) * LANES], LANES - DEC_SEQ, 1)
        else:
            nxt = tail
        new_ref[0, :, j * LANES:(j + 1) * LANES] = jnp.where(keep, cur, nxt)
        cur = nxt


def _sample_kernel(h_ref, ca_ref, c0_ref, c1_ref, c2_ref, sink_ref,
                   oa_ref, ob_ref, na_ref, n0_ref, n1_ref, n2_ref):
    lo = lax.broadcasted_iota(jnp.int32, (DEC_SEQ, LANES), 1) < HALF
    slabs = lambda name, count: [h_ref[_SLAB[name] + c] for c in range(count)]

    group = A_Q_HEADS // A_KV_HEADS
    pieces = []
    for pair, slab in enumerate(slabs("qa", A_WIDTH // LANES)):
        for sub in range(2):
            pieces.append(_to_half(slab * SCALE, sub, (2 * pair + sub) // group, lo))
    k_new, v_new = h_ref[_SLAB["ka"]], h_ref[_SLAB["va"]]
    o, _ = _sample_attend(jnp.concatenate(pieces, axis=0), ca_ref, k_new, v_new,
                          A_KV_WIDTH, 1, A_WINDOW - 1, sink_ref[...])
    for pair in range(A_Q_HEADS // 2):
        oa_ref[pair] = _merge_halves(o[(2 * pair) * DEC_SEQ:(2 * pair + 1) * DEC_SEQ, :],
                                     o[(2 * pair + 1) * DEC_SEQ:(2 * pair + 2) * DEC_SEQ, :],
                                     (2 * pair) // group, lo)
    _shift_cache(ca_ref, na_ref, jnp.concatenate([k_new, v_new], axis=1))

    n_col = B_KV_WIDTH // LANES
    zeros = jnp.zeros((DEC_SEQ, LANES), jnp.float32)
    outs, lses = [], []
    for gi, ((win, dil), c_ref, n_ref) in enumerate(
            zip(B_PAIRS, (c0_ref, c1_ref, c2_ref), (n0_ref, n1_ref, n2_ref))):
        pieces = []
        for g, slab in enumerate(slabs(f"qb{gi}", B_WIDTH // LANES)):
            for sub in range(2):
                cols = [zeros] * n_col
                cols[g // 2] = _to_half(slab * SCALE, sub, g % 2, lo)
                pieces.append(jnp.concatenate(cols, axis=1))
        k_new = jnp.concatenate(slabs(f"kb{gi}", n_col), axis=1)
        v_new = jnp.concatenate(slabs(f"vb{gi}", n_col), axis=1)
        o, lse = _sample_attend(jnp.concatenate(pieces, axis=0), c_ref, k_new, v_new,
                                B_KV_WIDTH, dil, win)
        outs.append(o), lses.append(lse)
        _shift_cache(c_ref, n_ref, jnp.concatenate([k_new, v_new], axis=1))
    top = functools.reduce(jnp.maximum, lses)
    es = [jnp.exp(v - top) for v in lses]
    tot = functools.reduce(lambda a, c: a + c, es)
    o = functools.reduce(lambda a, c: a + c, [(e / tot) * v for e, v in zip(es, outs)])
    for g in range(B_KV_HEADS):
        col = slice((g // 2) * LANES, (g // 2 + 1) * LANES)
        ob_ref[g] = _merge_halves(o[(2 * g) * DEC_SEQ:(2 * g + 1) * DEC_SEQ, col],
                                  o[(2 * g + 1) * DEC_SEQ:(2 * g + 2) * DEC_SEQ, col], g % 2, lo)


def _sample_step(hs3, caches_t, sink_rows):
    n = caches_t[0].shape[0]
    cache_specs = [pl.BlockSpec((1,) + c.shape[1:], lambda i: (i, 0, 0)) for c in caches_t]
    row_blk = lambda slabs: pl.BlockSpec((slabs, DEC_SEQ, LANES), lambda i: (0, i, 0))
    return pl.pallas_call(
        _sample_kernel,
        grid=(n,),
        in_specs=[row_blk(N_SLABS)] + cache_specs
        + [pl.BlockSpec((A_Q_HEADS * DEC_SEQ, 1), lambda i: (0, 0))],
        out_specs=[row_blk(A_WIDTH // LANES), row_blk(B_WIDTH // LANES)] + cache_specs,
        out_shape=[jax.ShapeDtypeStruct((A_WIDTH // LANES, n * DEC_SEQ, LANES), jnp.float32),
                   jax.ShapeDtypeStruct((B_WIDTH // LANES, n * DEC_SEQ, LANES), jnp.float32)]
        + [jax.ShapeDtypeStruct(c.shape, jnp.float32) for c in caches_t],
        compiler_params=_params(1),
        name="sample_step",
    )(hs3, *caches_t, sink_rows)


def _sigmoid(x):
    return 1.0 / (1.0 + jnp.exp(-x))


def _out_kernel(*refs, n_groups):
    x_ref, oa_ref, za_ref, ga_ref, gb_ref, zb_ref = refs[:6]
    ob_refs = refs[6:6 + n_groups]
    lse_refs = refs[6 + n_groups:6 + 2 * n_groups] if n_groups > 1 else ()
    wa_ref, wb_ref, wo_ref, lng_ref, lnb_ref, y_ref = refs[-6:]

    def gated(o, z):
        return (o * (z * _sigmoid(z))).astype(jnp.bfloat16)

    ya = jnp.concatenate([gated(oa_ref[s], za_ref[s]) for s in range(A_WIDTH // LANES)], axis=1)
    yb = []
    for s in range(B_WIDTH // LANES):
        if n_groups > 1:
            lses = [r[s] for r in lse_refs]
            top = functools.reduce(jnp.maximum, lses)
            es = [jnp.exp(v - top) for v in lses]
            tot = functools.reduce(lambda a, c: a + c, es)
            ob = functools.reduce(lambda a, c: a + c,
                                  [(e / tot) * r[s] for e, r in zip(es, ob_refs)])
        else:
            ob = ob_refs[0][s]
        yb.append(gated(ob, zb_ref[s]))
    yb = jnp.concatenate(yb, axis=1)
    da = jnp.dot(ya, wa_ref[...], preferred_element_type=jnp.float32)
    db = jnp.dot(yb, wb_ref[...], preferred_element_type=jnp.float32)
    m = jnp.concatenate(
        [(_sigmoid(ga_ref[s]) * da[:, s * LANES:(s + 1) * LANES]
          + _sigmoid(gb_ref[s]) * db[:, s * LANES:(s + 1) * LANES]).astype(jnp.bfloat16)
         for s in range(D_MODEL // LANES)], axis=1)
    h = DN_ALPHA * x_ref[...] + jnp.dot(m, wo_ref[...], preferred_element_type=jnp.float32)
    mu = jnp.mean(h, axis=-1, keepdims=True)
    var = jnp.mean(jnp.square(h - mu), axis=-1, keepdims=True)
    y_ref[...] = (h - mu) * lax.rsqrt(var + LN_EPS) * lng_ref[...] + lnb_ref[...]


def _out_proj(x, h3, oa, obs, lses, wa, wb, wo, ln_g, ln_b, tm=256):
    m = x.shape[0]
    n_groups = len(obs)

    def slab_blk(name_or_none, count):
        first = 0 if name_or_none is None else _SLAB[name_or_none] // count
        return pl.BlockSpec((count, tm, LANES), lambda i: (first, i, 0))

    row = pl.BlockSpec((tm, D_MODEL), lambda i: (i, 0))
    whole = lambda a: pl.BlockSpec(a.shape, lambda i: (0, 0), pipeline_mode=pl.Buffered(1))
    a_slabs, b_slabs, d_slabs = A_WIDTH // LANES, B_WIDTH // LANES, D_MODEL // LANES
    in_specs = ([row, slab_blk(None, a_slabs), slab_blk("za", a_slabs), slab_blk("ga", d_slabs),
                 slab_blk("gb", d_slabs), slab_blk("zb", b_slabs)]
                + [slab_blk(None, b_slabs)] * (len(obs) + len(lses))
                + [whole(wa), whole(wb), whole(wo), whole(ln_g), whole(ln_b)])
    return pl.pallas_call(
        functools.partial(_out_kernel, n_groups=n_groups),
        grid=(m // tm,),
        in_specs=in_specs,
        out_specs=row,
        out_shape=jax.ShapeDtypeStruct((m, D_MODEL), jnp.float32),
        compiler_params=_params(1),
        name=f"out_proj{n_groups}",
    )(x, oa, h3, h3, h3, h3, *obs, *lses, wa, wb, wo, ln_g, ln_b)


def _permute_cols(a):
    return jnp.concatenate([a[..., _ORIG[s][0]:_ORIG[s][0] + _ORIG[s][1]] for s in _SEGS], axis=-1)


def kernel(x_prompt, x_sample, cache_a_kv, cache_b0_kv, cache_b1_kv, cache_b2_kv,
           w_in, b_in, sink_a, w_br_a, w_br_b, w_out, ln_g, ln_b):
    n, l, _ = x_prompt.shape
    ns, t, _ = x_sample.shape
    assert w_in.shape[0] == 1 and t == DEC_SEQ and l == SEQ
    assert PAST_LEN >= max(w for w, _ in B_PAIRS)
    w_bf = _permute_cols(w_in[0]).astype(jnp.bfloat16)
    b_p = _permute_cols(b_in[0])[None, :]
    wa, wb, wo = (w[0].astype(jnp.bfloat16) for w in (w_br_a, w_br_b, w_out))
    tabs_p = _rope_tables(jnp.arange(l, dtype=jnp.float32))
    tabs_s = [jnp.tile(tb, (ns, 1)) for tb in
              _rope_tables(PAST_LEN + jnp.arange(t, dtype=jnp.float32))]
    sink = sink_a[0]
    sink_rows = jnp.repeat(sink, DEC_SEQ)[:, None]

    xp = x_prompt.reshape(n * l, D_MODEL)
    hp = _in_proj(xp, w_bf, b_p, tabs_p)
    oa = _attn_a(hp, sink, n, l)
    st_a = _kv_state(hp, "ka", "va", A_KV_WIDTH, min(A_WINDOW, l), n, l, "state_a")
    obs, lses, st_b = [], [], []
    for gi, (win, dil) in enumerate(B_PAIRS):
        o, lse = _attn_b(hp, gi, dil, n, l, n_sub={1: 4, 4: 2, 16: 1}[dil])
        obs.append(o), lses.append(lse)
        st_b.append(_kv_state(hp, f"kb{gi}", f"vb{gi}", B_KV_WIDTH, min(win, l), n, l,
                              f"state_b{gi}"))
    yp = _out_proj(xp, hp, oa, obs, lses, wa, wb, wo, ln_g, ln_b)

    xs = x_sample.reshape(ns * t, D_MODEL)
    hs = _in_proj(xs, w_bf, b_p, tabs_s)
    to_t = lambda c, w: jnp.transpose(c[0].reshape(ns, -1, w), (0, 2, 1))
    caches_t = [to_t(cache_a_kv, 2 * A_KV_WIDTH)] + [
        to_t(c, 2 * B_KV_WIDTH) for c in (cache_b0_kv, cache_b1_kv, cache_b2_kv)]
    oa_s, ob_s, *new_t = _sample_step(hs, caches_t, sink_rows)
    ys = _out_proj(xs, hs, oa_s, [ob_s], [], wa, wb, wo, ln_g, ln_b)

    kv_shape = lambda a, heads: a.reshape((1, a.shape[0], a.shape[1], 2, heads, HEAD_DIM))
    from_t = lambda a, heads: kv_shape(jnp.transpose(a, (0, 2, 1)), heads)
    return (yp.reshape(n, l, D_MODEL), ys.reshape(ns, t, D_MODEL),
            from_t(st_a, A_KV_HEADS), *[from_t(s, B_KV_HEADS) for s in st_b],
            from_t(new_t[0], A_KV_HEADS), *[from_t(c, B_KV_HEADS) for c in new_t[1:]])
```

```python
import functools

import jax
import jax.numpy as jnp
import numpy as np
from jax import lax
from jax.experimental import pallas as pl
from jax.experimental.pallas import tpu as pltpu

D_MODEL = 2048
SEQ = 8192
DEC_SEQ = 8
PAST_LEN = 16384
HEAD_DIM = 64
ROT_DIM = HEAD_DIM // 4
ROPE_THETA = 500000.0
BLK = 128
A_WINDOW = 128
A_Q_HEADS = 16
A_KV_HEADS = 2
B_PAIRS = ((128, 1), (512, 4), (2048, 16))
B_Q_HEADS = 8
B_KV_HEADS = 4
A_WIDTH = A_Q_HEADS * HEAD_DIM
B_WIDTH = B_Q_HEADS * HEAD_DIM
A_KV_WIDTH = A_KV_HEADS * HEAD_DIM
B_KV_WIDTH = B_KV_HEADS * HEAD_DIM
SCALE = HEAD_DIM ** -0.5
DN_ALPHA = 2.0 ** 0.25
LN_EPS = 1e-5
LANES = 128
HALF = LANES // 2

_ORIG = {}
_off = 0
for _name, _w in ([("qa", A_WIDTH), ("ka", A_KV_WIDTH), ("va", A_KV_WIDTH), ("za", A_WIDTH)]
                  + [(f"{p}b{i}", w) for i in range(3)
                     for p, w in (("q", B_WIDTH), ("k", B_KV_WIDTH), ("v", B_KV_WIDTH))]
                  + [("zb", B_WIDTH), ("ga", D_MODEL), ("gb", D_MODEL)]):
    _ORIG[_name] = (_off, _w)
    _off += _w
IN_WIDTH = _off
N_SLABS = IN_WIDTH // LANES

_SEGS = ["qa", "qb0", "qb1", "qb2", "kb0", "kb1", "kb2", "ka", "va",
         "vb0", "vb1", "ga", "gb", "za", "zb", "vb2"]
_SLAB = {}
_off = 0
for _name in _SEGS:
    _SLAB[_name] = _off // LANES
    assert _SLAB[_name] % min(_ORIG[_name][1] // LANES, 16) == 0 or _name in ("ka", "va")
    _off += _ORIG[_name][1]
assert _off == IN_WIDTH and _SLAB["va"] == _SLAB["ka"] + 1 and _SLAB["ka"] % 2 == 0
N_ROPE_SLABS = _SLAB["va"]

_VMEM_LIMIT = 56 * 1024 * 1024


def _params(n_axes, vmem=_VMEM_LIMIT):
    return pltpu.CompilerParams(dimension_semantics=("arbitrary",) * n_axes,
                                vmem_limit_bytes=vmem)


def _rope_tables(pos):
    half = ROT_DIM // 2
    inv = ROPE_THETA ** (-jnp.arange(0, ROT_DIM, 2, dtype=jnp.float32) / ROT_DIM)
    dim = np.arange(LANES) % HEAD_DIM
    ang = pos[:, None] * inv[dim % half][None, :]
    cos, sin = jnp.cos(ang), jnp.sin(ang)
    first, second = (dim < half)[None, :], ((dim >= half) & (dim < ROT_DIM))[None, :]
    c = jnp.where(first | second, cos, 1.0)
    s1 = jnp.where(first, -sin, 0.0)
    s2 = jnp.where(second, sin, 0.0)
    return c, s1, s2


def _rope(h, c, s1, s2):
    half = ROT_DIM // 2
    return h * c + pltpu.roll(h, LANES - half, 1) * s1 + pltpu.roll(h, half, 1) * s2


def _to_half(piece, src_half, dst_half, lo):
    if src_half != dst_half:
        piece = pltpu.roll(piece, HALF, 1)
    keep = lo if dst_half == 0 else jnp.logical_not(lo)
    return jnp.where(keep, piece, 0.0)


def _merge_halves(even, odd, src_half, lo):
    if src_half == 1:
        even = pltpu.roll(even, HALF, 1)
    else:
        odd = pltpu.roll(odd, HALF, 1)
    return jnp.where(lo, even, odd)


def _nt(a, b):
    return lax.dot_general(a, b, (((1,), (1,)), ((), ())), preferred_element_type=jnp.float32)


def _band_bias(lo_off, hi_off, no_previous):
    i = lax.broadcasted_iota(jnp.int32, (BLK, 2 * BLK), 0)
    j = lax.broadcasted_iota(jnp.int32, (BLK, 2 * BLK), 1)
    band = (j >= i + lo_off) & (j <= i + hi_off)
    first_key = jnp.where(no_previous, BLK, 0)
    return (jnp.where(band, 0.0, -jnp.inf), jnp.where(band & (j >= first_key), 0.0, -jnp.inf))


def _softmax_pv(s, bias, v_win, sink=None):
    s = s + bias
    m = jnp.max(s, axis=1, keepdims=True)
    if sink is not None:
        m = jnp.maximum(m, sink)
    p = jnp.exp(s - m)
    den = jnp.sum(p, axis=1, keepdims=True)
    if sink is not None:
        den = den + jnp.exp(sink - m)
    o = jnp.dot(p.astype(jnp.bfloat16), v_win, preferred_element_type=jnp.float32)
    return o * (1.0 / den), m + jnp.log(den)


def _in_proj_kernel(x_ref, w_ref, b_ref, c_ref, s1_ref, s2_ref, o_ref, xb_ref, *, slabs):
    j = pl.program_id(1)

    @pl.when(j == 0)
    def _():
        xb_ref[...] = x_ref[...].astype(jnp.bfloat16)

    def tile(rope_slab):
        for c in range(slabs // 2):
            cols = slice(c * 2 * LANES, (c + 1) * 2 * LANES)
            acc = jnp.dot(xb_ref[...], w_ref[:, cols],
                          preferred_element_type=jnp.float32) + b_ref[:, cols]
            for t in range(2):
                col = acc[:, t * LANES:(t + 1) * LANES]
                if rope_slab[2 * c + t]:
                    col = _rope(col, c_ref[...], s1_ref[...], s2_ref[...])
                o_ref[2 * c + t] = col

    full, rest = divmod(N_ROPE_SLABS, slabs)
    pl.when(j < full)(lambda: tile([True] * slabs))
    if rest:
        pl.when(j == full)(lambda: tile([s < rest for s in range(slabs)]))
    pl.when(j >= full + (1 if rest else 0))(lambda: tile([False] * slabs))


def _in_proj(x, w_bf, b, tabs, tm=1024, tn=768):
    m, k = x.shape
    n = w_bf.shape[1]
    n_pos_blocks = tabs[0].shape[0] // tm
    tab = pl.BlockSpec((tm, LANES), lambda i, j: (i % n_pos_blocks, 0))
    return pl.pallas_call(
        functools.partial(_in_proj_kernel, slabs=tn // LANES),
        grid=(m // tm, n // tn),
        in_specs=[pl.BlockSpec((tm, k), lambda i, j: (i, 0)),
                  pl.BlockSpec((k, tn), lambda i, j: (0, j)),
                  pl.BlockSpec((1, tn), lambda i, j: (0, j)), tab, tab, tab],
        out_specs=pl.BlockSpec((tn // LANES, tm, LANES), lambda i, j: (j, i, 0)),
        out_shape=jax.ShapeDtypeStruct((n // LANES, m, LANES), jnp.float32),
        scratch_shapes=[pltpu.VMEM((tm, k), jnp.bfloat16)],
        compiler_params=_params(2),
        name="in_proj",
    )(x, w_bf, b, *tabs)


_UNITS_PER_TRIP = 2


def _attend_pairs(slabs, k_wins, v_wins, kv_of_pair, bias, lo, sinks=None):
    result = []
    for p, slab in enumerate(slabs):
        col, half = kv_of_pair[p]
        slab = slab * SCALE
        outs, lses = [], []
        for sub in range(2):
            q_h = _to_half(slab, sub, half, lo).astype(jnp.bfloat16)
            o, lse = _softmax_pv(_nt(q_h, k_wins[col]), bias, v_wins[col],
                                 None if sinks is None else sinks[2 * p + sub])
            outs.append(o), lses.append(lse)
        result.append((_merge_halves(outs[0], outs[1], half, lo),
                       jnp.where(lo, jnp.broadcast_to(lses[0], (BLK, LANES)),
                                 jnp.broadcast_to(lses[1], (BLK, LANES)))))
    return result


def _attn_a_kernel(sink_ref, q_ref, kv_ref, kvp_ref, o_ref, k_buf, v_buf, *, n_sub):
    lo = lax.broadcasted_iota(jnp.int32, (BLK, LANES), 1) < HALF
    seq_start = pl.program_id(1) == 0
    group = A_Q_HEADS // A_KV_HEADS
    for buf, c in ((k_buf, 0), (v_buf, 1)):
        buf[0:BLK, :] = kvp_ref[c].astype(jnp.bfloat16)
        buf[BLK:, :] = kv_ref[c].astype(jnp.bfloat16)
    sinks = [sink_ref[h] for h in range(A_Q_HEADS)]
    kv_of_pair = [(0, (2 * p) // group) for p in range(A_Q_HEADS // 2)]

    def block(j, _):
        start = pl.multiple_of(j * BLK, BLK)
        rows = pl.ds(start, BLK)
        window = pl.ds(start, 2 * BLK)
        _, bias = _band_bias(1, A_WINDOW, seq_start & (j == 0))
        outs = _attend_pairs([q_ref[p, rows, :] for p in range(A_Q_HEADS // 2)],
                             [k_buf[window, :]], [v_buf[window, :]], kv_of_pair, bias, lo, sinks)
        for p, (o, _) in enumerate(outs):
            o_ref[p, rows, :] = o

    lax.fori_loop(0, n_sub, block, None)


def _attn_a(h3, sink, n, l, n_sub=4):
    rows = BLK * n_sub
    nsb = l // rows
    q_slabs = A_WIDTH // LANES
    tok = lambda i, sb: i * nsb + sb
    prev_blk = lambda i, sb: jnp.maximum(tok(i, sb) * n_sub - 1, 0)
    return pl.pallas_call(
        functools.partial(_attn_a_kernel, n_sub=n_sub),
        grid=(n, nsb),
        in_specs=[pl.BlockSpec(memory_space=pltpu.SMEM),
                  pl.BlockSpec((q_slabs, rows, LANES),
                               lambda i, sb: (_SLAB["qa"] // q_slabs, tok(i, sb), 0)),
                  pl.BlockSpec((2, rows, LANES), lambda i, sb: (_SLAB["ka"] // 2, tok(i, sb), 0)),
                  pl.BlockSpec((2, BLK, LANES),
                               lambda i, sb: (_SLAB["ka"] // 2, prev_blk(i, sb), 0))],
        out_specs=pl.BlockSpec((q_slabs, rows, LANES), lambda i, sb: (0, tok(i, sb), 0)),
        out_shape=jax.ShapeDtypeStruct((q_slabs, n * l, LANES), jnp.float32),
        scratch_shapes=[pltpu.VMEM((rows + BLK, LANES), jnp.bfloat16)] * 2,
        compiler_params=_params(2),
        name="attn_a",
    )(sink, h3, h3, h3)


def _attn_b_kernel(q_ref, k_ref, v_ref, kp_ref, vp_ref, o_ref, lse_ref, k_buf, v_buf, *,
                   dil, n_sub):
    n_col = B_KV_WIDTH // LANES
    lo = lax.broadcasted_iota(jnp.int32, (BLK, LANES), 1) < HALF
    seq_start = pl.program_id(1) == 0
    kv_of_pair = [(g // 2, g % 2) for g in range(B_KV_HEADS)]

    def fold(r, j):
        if dil == 1:
            return pl.ds(pl.multiple_of(j * BLK, BLK), BLK)
        return pl.ds(j * (BLK * dil) + r, BLK, stride=dil)

    def fill(r, _):
        for buf, prev, cur in ((k_buf, kp_ref, k_ref), (v_buf, vp_ref, v_ref)):
            for c in range(n_col):
                buf[r, c, 0:BLK, :] = prev[c, fold(r, 0), :].astype(jnp.bfloat16)
                for j in range(n_sub):
                    buf[r, c, (j + 1) * BLK:(j + 2) * BLK, :] = (
                        cur[c, fold(r, j), :].astype(jnp.bfloat16))

    def block(u, _):
        r, j = u // n_sub, u % n_sub
        window = pl.ds(pl.multiple_of(j * BLK, BLK), 2 * BLK)
        _, bias = _band_bias(0, BLK, seq_start & (j == 0))
        outs = _attend_pairs([q_ref[g, fold(r, j), :] for g in range(B_KV_HEADS)],
                             [k_buf[r, c, window, :] for c in range(n_col)],
                             [v_buf[r, c, window, :] for c in range(n_col)], kv_of_pair, bias, lo)
        for g, (o, lse) in enumerate(outs):
            o_ref[g, fold(r, j), :] = o
            lse_ref[g, fold(r, j), :] = lse

    if dil == 1:
        fill(0, None)
    else:
        lax.fori_loop(0, dil, fill, None)
    lax.fori_loop(0, dil * n_sub, block, None, unroll=_UNITS_PER_TRIP)


def _attn_b(h3, gi, dil, n, l, n_sub):
    prev_rows = BLK * dil
    rows = prev_rows * n_sub
    nsb = l // rows
    q_slabs = B_WIDTH // LANES
    kv_slabs = B_KV_WIDTH // LANES
    tok = lambda i, sb: i * nsb + sb
    prev_blk = lambda i, sb: jnp.maximum(tok(i, sb) * n_sub - 1, 0)
    out_blk = pl.BlockSpec((q_slabs, rows, LANES), lambda i, sb: (0, tok(i, sb), 0))

    prev_mode = dict(pipeline_mode=pl.Buffered(1)) if dil > 4 else {}

    def kv_specs(name):
        first = _SLAB[f"{name}{gi}"] // kv_slabs
        return (pl.BlockSpec((kv_slabs, rows, LANES), lambda i, sb: (first, tok(i, sb), 0)),
                pl.BlockSpec((kv_slabs, prev_rows, LANES),
                             lambda i, sb: (first, prev_blk(i, sb), 0), **prev_mode))

    (k_cur, k_prev), (v_cur, v_prev) = kv_specs("kb"), kv_specs("vb")
    return pl.pallas_call(
        functools.partial(_attn_b_kernel, dil=dil, n_sub=n_sub),
        grid=(n, nsb),
        in_specs=[pl.BlockSpec((q_slabs, rows, LANES),
                               lambda i, sb: (_SLAB[f"qb{gi}"] // q_slabs, tok(i, sb), 0)),
                  k_cur, v_cur, k_prev, v_prev],
        out_specs=[out_blk, out_blk],
        out_shape=[jax.ShapeDtypeStruct((q_slabs, n * l, LANES), jnp.float32)] * 2,
        scratch_shapes=[pltpu.VMEM((dil, kv_slabs, (n_sub + 1) * BLK, LANES), jnp.bfloat16)] * 2,
        compiler_params=_params(2),
        name=f"attn_b{gi}",
    )(h3, h3, h3, h3, h3)


def _kv_state_kernel(k_ref, v_ref, o_ref):
    n_k = k_ref.shape[0]
    for c in range(n_k):
        o_ref[0, c * LANES:(c + 1) * LANES, :] = jnp.transpose(k_ref[c])
        o_ref[0, (n_k + c) * LANES:(n_k + c + 1) * LANES, :] = jnp.transpose(v_ref[c])


def _kv_state(h3, k_name, v_name, width, win, n, l, name):
    slabs = width // LANES
    chunk = min(win, 4 * BLK)
    per_seq = win // chunk
    first_chunk = lambda i, t: ((i + 1) * l - win) // chunk + t
    spec = lambda s: pl.BlockSpec((slabs, chunk, LANES),
                                  lambda i, t: (_SLAB[s] // slabs, first_chunk(i, t), 0))
    return pl.pallas_call(
        _kv_state_kernel,
        grid=(n, per_seq),
        in_specs=[spec(k_name), spec(v_name)],
        out_specs=pl.BlockSpec((1, 2 * width, chunk), lambda i, t: (i, 0, t)),
        out_shape=jax.ShapeDtypeStruct((n, 2 * width, win), jnp.float32),
        compiler_params=_params(2),
        name=name,
    )(h3, h3)


_PAD = 16
_SAMPLE_SEQS_PER_STEP = 2


def _pad_rows(x, rows=_PAD, at=0):
    parts = []
    if at:
        parts.append(jnp.zeros((at, x.shape[1]), x.dtype))
    parts.append(x)
    if rows - at - x.shape[0]:
        parts.append(jnp.zeros((rows - at - x.shape[0], x.shape[1]), x.dtype))
    return jnp.concatenate(parts, axis=0)


def _sample_attend(qz, cache_ref, b, k_new, v_new, width, dil, max_dist, sink=None):
    cb = cache_ref.shape[2]
    qz = qz.astype(jnp.bfloat16)
    s_c = jnp.dot(qz, cache_ref[b, 0:width, :].astype(jnp.bfloat16),
                  preferred_element_type=jnp.float32)
    s_n = _nt(qz, _pad_rows(k_new).astype(jnp.bfloat16))
    parts = []
    for s, base in ((s_c, 0), (s_n, cb)):
        t = lax.broadcasted_iota(jnp.int32, s.shape, 0) & (DEC_SEQ - 1)
        r = lax.broadcasted_iota(jnp.int32, s.shape, 1) + base
        dist = cb + t - r
        valid = (dist >= 0) & (dist <= max_dist) & ((dist & (dil - 1)) == 0)
        parts.append(jnp.where(valid, s, -jnp.inf))
    s_c, s_n = parts
    m = jnp.maximum(jnp.max(s_c, axis=1, keepdims=True), jnp.max(s_n, axis=1, keepdims=True))
    if sink is not None:
        m = jnp.maximum(m, sink)
    p_c, p_n = jnp.exp(s_c - m), jnp.exp(s_n - m)
    den = jnp.sum(p_c, axis=1, keepdims=True) + jnp.sum(p_n, axis=1, keepdims=True)
    if sink is not None:
        den = den + jnp.exp(sink - m)
    o = (_nt(p_c.astype(jnp.bfloat16), cache_ref[b, width:2 * width, :].astype(jnp.bfloat16))
         + jnp.dot(p_n.astype(jnp.bfloat16), _pad_rows(v_new).astype(jnp.bfloat16),
                   preferred_element_type=jnp.float32))
    return o / den, m + jnp.log(den)


def _shift_cache(cache_ref, new_ref, b, new_rows):
    cb = cache_ref.shape[2]
    lane = lax.broadcasted_iota(jnp.int32, (cache_ref.shape[1], LANES), 1)
    keep = lane < LANES - DEC_SEQ
    tail = jnp.transpose(_pad_rows(new_rows, LANES, LANES - DEC_SEQ))
    cur = pltpu.roll(cache_ref[b, :, 0:LANES], LANES - DEC_SEQ, 1)
    for j in range(cb // LANES):
        if j + 1 < cb // LANES:
            nxt = pltpu.roll(cache_ref[b, :, (j + 1) * LANES:(j + 2) * LANES], LANES - DEC_SEQ, 1)
        else:
            nxt = tail
        new_ref[b, :, j * LANES:(j + 1) * LANES] = jnp.where(keep, cur, nxt)
        cur = nxt


def _sample_kernel(*refs):
    for b in range(_SAMPLE_SEQS_PER_STEP):
        _sample_one(b, *refs)


def _sample_one(b, h_ref, ca_ref, c0_ref, c1_ref, c2_ref, sink_ref,
                oa_ref, ob_ref, na_ref, n0_ref, n1_ref, n2_ref):
    lo = lax.broadcasted_iota(jnp.int32, (DEC_SEQ, LANES), 1) < HALF
    rows = slice(b * DEC_SEQ, (b + 1) * DEC_SEQ)
    slabs = lambda name, count: [h_ref[_SLAB[name] + c, rows, :] for c in range(count)]

    group = A_Q_HEADS // A_KV_HEADS
    pieces = []
    for pair, slab in enumerate(slabs("qa", A_WIDTH // LANES)):
        for sub in range(2):
            pieces.append(_to_half(slab * SCALE, sub, (2 * pair + sub) // group, lo))
    (k_new,), (v_new,) = slabs("ka", 1), slabs("va", 1)
    o, _ = _sample_attend(jnp.concatenate(pieces, axis=0), ca_ref, b, k_new, v_new,
                          A_KV_WIDTH, 1, A_WINDOW - 1, sink_ref[...])
    for pair in range(A_Q_HEADS // 2):
        oa_ref[pair, rows, :] = _merge_halves(o[(2 * pair) * DEC_SEQ:(2 * pair + 1) * DEC_SEQ, :],
                                     o[(2 * pair + 1) * DEC_SEQ:(2 * pair + 2) * DEC_SEQ, :],
                                     (2 * pair) // group, lo)
    _shift_cache(ca_ref, na_ref, b, jnp.concatenate([k_new, v_new], axis=1))

    n_col = B_KV_WIDTH // LANES
    zeros = jnp.zeros((DEC_SEQ, LANES), jnp.float32)
    outs, lses = [], []
    for gi, ((win, dil), c_ref, n_ref) in enumerate(
            zip(B_PAIRS, (c0_ref, c1_ref, c2_ref), (n0_ref, n1_ref, n2_ref))):
        pieces = []
        for g, slab in enumerate(slabs(f"qb{gi}", B_WIDTH // LANES)):
            for sub in range(2):
                cols = [zeros] * n_col
                cols[g // 2] = _to_half(slab * SCALE, sub, g % 2, lo)
                pieces.append(jnp.concatenate(cols, axis=1))
        k_new = jnp.concatenate(slabs(f"kb{gi}", n_col), axis=1)
        v_new = jnp.concatenate(slabs(f"vb{gi}", n_col), axis=1)
        o, lse = _sample_attend(jnp.concatenate(pieces, axis=0), c_ref, b, k_new, v_new,
                                B_KV_WIDTH, dil, win)
        outs.append(o), lses.append(lse)
        _shift_cache(c_ref, n_ref, b, jnp.concatenate([k_new, v_new], axis=1))
    top = functools.reduce(jnp.maximum, lses)
    es = [jnp.exp(v - top) for v in lses]
    tot = functools.reduce(lambda a, c: a + c, es)
    o = functools.reduce(lambda a, c: a + c, [(e / tot) * v for e, v in zip(es, outs)])
    for g in range(B_KV_HEADS):
        col = slice((g // 2) * LANES, (g // 2 + 1) * LANES)
        ob_ref[g, rows, :] = _merge_halves(o[(2 * g) * DEC_SEQ:(2 * g + 1) * DEC_SEQ, col],
                                  o[(2 * g + 1) * DEC_SEQ:(2 * g + 2) * DEC_SEQ, col], g % 2, lo)


def _sample_step(hs3, caches_t, sink_rows):
    n = caches_t[0].shape[0]
    per_step = _SAMPLE_SEQS_PER_STEP
    cache_specs = [pl.BlockSpec((per_step,) + c.shape[1:], lambda i: (i, 0, 0)) for c in caches_t]
    row_blk = lambda slabs: pl.BlockSpec((slabs, per_step * DEC_SEQ, LANES), lambda i: (0, i, 0))
    return pl.pallas_call(
        _sample_kernel,
        grid=(n // per_step,),
        in_specs=[row_blk(N_SLABS)] + cache_specs
        + [pl.BlockSpec((A_Q_HEADS * DEC_SEQ, 1), lambda i: (0, 0))],
        out_specs=[row_blk(A_WIDTH // LANES), row_blk(B_WIDTH // LANES)] + cache_specs,
        out_shape=[jax.ShapeDtypeStruct((A_WIDTH // LANES, n * DEC_SEQ, LANES), jnp.float32),
                   jax.ShapeDtypeStruct((B_WIDTH // LANES, n * DEC_SEQ, LANES), jnp.float32)]
        + [jax.ShapeDtypeStruct(c.shape, jnp.float32) for c in caches_t],
        compiler_params=_params(1),
        name="sample_step",
    )(hs3, *caches_t, sink_rows)


def _sigmoid(x):
    return 1.0 / (1.0 + jnp.exp(-x))


def _out_kernel(*refs, n_groups):
    x_ref, oa_ref, za_ref, ga_ref, gb_ref, zb_ref = refs[:6]
    ob_refs = refs[6:6 + n_groups]
    lse_refs = refs[6 + n_groups:6 + 2 * n_groups] if n_groups > 1 else ()
    wa_ref, wb_ref, wo_ref, lng_ref, lnb_ref, y_ref = refs[-6:]

    def gated(o, z):
        return (o * (z * _sigmoid(z))).astype(jnp.bfloat16)

    ya = jnp.concatenate([gated(oa_ref[s], za_ref[s]) for s in range(A_WIDTH // LANES)], axis=1)
    yb = []
    for s in range(B_WIDTH // LANES):
        if n_groups > 1:
            lses = [r[s] for r in lse_refs]
            top = functools.reduce(jnp.maximum, lses)
            es = [jnp.exp(v - top) for v in lses]
            tot = functools.reduce(lambda a, c: a + c, es)
            ob = functools.reduce(lambda a, c: a + c,
                                  [(e / tot) * r[s] for e, r in zip(es, ob_refs)])
        else:
            ob = ob_refs[0][s]
        yb.append(gated(ob, zb_ref[s]))
    yb = jnp.concatenate(yb, axis=1)
    da = jnp.dot(ya, wa_ref[...], preferred_element_type=jnp.float32)
    db = jnp.dot(yb, wb_ref[...], preferred_element_type=jnp.float32)
    m = jnp.concatenate(
        [(_sigmoid(ga_ref[s]) * da[:, s * LANES:(s + 1) * LANES]
          + _sigmoid(gb_ref[s]) * db[:, s * LANES:(s + 1) * LANES]).astype(jnp.bfloat16)
         for s in range(D_MODEL // LANES)], axis=1)
    h = DN_ALPHA * x_ref[...] + jnp.dot(m, wo_ref[...], preferred_element_type=jnp.float32)
    mu = jnp.mean(h, axis=-1, keepdims=True)
    var = jnp.mean(jnp.square(h - mu), axis=-1, keepdims=True)
    y_ref[...] = (h - mu) * lax.rsqrt(var + LN_EPS) * lng_ref[...] + lnb_ref[...]


def _out_proj(x, h3, oa, obs, lses, wa, wb, wo, ln_g, ln_b, tm=256):
    m = x.shape[0]
    n_groups = len(obs)

    def slab_blk(name_or_none, count):
        first = 0 if name_or_none is None else _SLAB[name_or_none] // count
        return pl.BlockSpec((count, tm, LANES), lambda i: (first, i, 0))

    row = pl.BlockSpec((tm, D_MODEL), lambda i: (i, 0))
    whole = lambda a: pl.BlockSpec(a.shape, lambda i: (0, 0), pipeline_mode=pl.Buffered(1))
    a_slabs, b_slabs, d_slabs = A_WIDTH // LANES, B_WIDTH // LANES, D_MODEL // LANES
    in_specs = ([row, slab_blk(None, a_slabs), slab_blk("za", a_slabs), slab_blk("ga", d_slabs),
                 slab_blk("gb", d_slabs), slab_blk("zb", b_slabs)]
                + [slab_blk(None, b_slabs)] * (len(obs) + len(lses))
                + [whole(wa), whole(wb), whole(wo), whole(ln_g), whole(ln_b)])
    return pl.pallas_call(
        functools.partial(_out_kernel, n_groups=n_groups),
        grid=(m // tm,),
        in_specs=in_specs,
        out_specs=row,
        out_shape=jax.ShapeDtypeStruct((m, D_MODEL), jnp.float32),
        compiler_params=_params(1),
        name=f"out_proj{n_groups}",
    )(x, oa, h3, h3, h3, h3, *obs, *lses, wa, wb, wo, ln_g, ln_b)


_UNIT = 2 * LANES
_COL_ORDER = np.concatenate([np.arange(_ORIG[s][0], _ORIG[s][0] + _ORIG[s][1]) for s in _SEGS])
assert all(_COL_ORDER[u * _UNIT] % _UNIT == 0
           and np.all(np.diff(_COL_ORDER[u * _UNIT:(u + 1) * _UNIT]) == 1)
           for u in range(IN_WIDTH // _UNIT))


def _cast_permute_kernel(order_ref, w_ref, o_ref):
    del order_ref
    o_ref[...] = w_ref[...].astype(jnp.bfloat16)


def _cast_permute_w(w):
    k, n = w.shape
    unit_order = jnp.asarray(_COL_ORDER[::_UNIT] // _UNIT, jnp.int32)
    return pl.pallas_call(
        _cast_permute_kernel,
        grid_spec=pltpu.PrefetchScalarGridSpec(
            num_scalar_prefetch=1, grid=(n // _UNIT,),
            in_specs=[pl.BlockSpec((k, _UNIT), lambda j, order: (0, order[j]))],
            out_specs=pl.BlockSpec((k, _UNIT), lambda j, order: (0, j))),
        out_shape=jax.ShapeDtypeStruct((k, n), jnp.bfloat16),
        compiler_params=_params(1),
        name="cast_permute_w",
    )(unit_order, w)


def kernel(x_prompt, x_sample, cache_a_kv, cache_b0_kv, cache_b1_kv, cache_b2_kv,
           w_in, b_in, sink_a, w_br_a, w_br_b, w_out, ln_g, ln_b):
    n, l, _ = x_prompt.shape
    ns, t, _ = x_sample.shape
    assert w_in.shape[0] == 1 and t == DEC_SEQ and l == SEQ
    assert PAST_LEN >= max(w for w, _ in B_PAIRS)
    w_bf = _cast_permute_w(w_in[0])
    b_p = jnp.take(b_in[0], jnp.asarray(_COL_ORDER, jnp.int32))[None, :]
    wa, wb, wo = (w[0].astype(jnp.bfloat16) for w in (w_br_a, w_br_b, w_out))
    tabs_p = _rope_tables(jnp.arange(l, dtype=jnp.float32))
    tabs_s = [jnp.tile(tb, (ns, 1)) for tb in
              _rope_tables(PAST_LEN + jnp.arange(t, dtype=jnp.float32))]
    sink = sink_a[0]
    sink_rows = jnp.repeat(sink, DEC_SEQ)[:, None]

    xp = x_prompt.reshape(n * l, D_MODEL)
    hp = _in_proj(xp, w_bf, b_p, tabs_p)
    oa = _attn_a(hp, sink, n, l)
    st_a = _kv_state(hp, "ka", "va", A_KV_WIDTH, min(A_WINDOW, l), n, l, "state_a")
    obs, lses, st_b = [], [], []
    for gi, (win, dil) in enumerate(B_PAIRS):
        o, lse = _attn_b(hp, gi, dil, n, l, n_sub={1: 4, 4: 2, 16: 1}[dil])
        obs.append(o), lses.append(lse)
        st_b.append(_kv_state(hp, f"kb{gi}", f"vb{gi}", B_KV_WIDTH, min(win, l), n, l,
                              f"state_b{gi}"))
    yp = _out_proj(xp, hp, oa, obs, lses, wa, wb, wo, ln_g, ln_b)

    xs = x_sample.reshape(ns * t, D_MODEL)
    hs = _in_proj(xs, w_bf, b_p, tabs_s)
    to_t = lambda c, w: jnp.transpose(c[0].reshape(ns, -1, w), (0, 2, 1))
    caches_t = [to_t(cache_a_kv, 2 * A_KV_WIDTH)] + [
        to_t(c, 2 * B_KV_WIDTH) for c in (cache_b0_kv, cache_b1_kv, cache_b2_kv)]
    oa_s, ob_s, *new_t = _sample_step(hs, caches_t, sink_rows)
    ys = _out_proj(xs, hs, oa_s, [ob_s], [], wa, wb, wo, ln_g, ln_b)

    kv_shape = lambda a, heads: a.reshape((1, a.shape[0], a.shape[1], 2, heads, HEAD_DIM))
    from_t = lambda a, heads: kv_shape(jnp.transpose(a, (0, 2, 1)), heads)
    return (yp.reshape(n, l, D_MODEL), ys.reshape(ns, t, D_MODEL),
            from_t(st_a, A_KV_HEADS), *[from_t(s, B_KV_HEADS) for s in st_b],
            from_t(new_t[0], A_KV_HEADS), *[from_t(c, B_KV_HEADS) for c in new_t[1:]])
```

```python
import functools

import jax
import jax.numpy as jnp
import numpy as np
from jax import lax
from jax.experimental import pallas as pl
from jax.experimental.pallas import tpu as pltpu

D_MODEL = 2048
SEQ = 8192
DEC_SEQ = 8
PAST_LEN = 16384
HEAD_DIM = 64
ROT_DIM = HEAD_DIM // 4
ROPE_THETA = 500000.0
BLK = 128
A_WINDOW = 128
A_Q_HEADS = 16
A_KV_HEADS = 2
B_PAIRS = ((128, 1), (512, 4), (2048, 16))
B_Q_HEADS = 8
B_KV_HEADS = 4
A_WIDTH = A_Q_HEADS * HEAD_DIM
B_WIDTH = B_Q_HEADS * HEAD_DIM
A_KV_WIDTH = A_KV_HEADS * HEAD_DIM
B_KV_WIDTH = B_KV_HEADS * HEAD_DIM
SCALE = HEAD_DIM ** -0.5
DN_ALPHA = 2.0 ** 0.25
LN_EPS = 1e-5
LANES = 128
HALF = LANES // 2

_ORIG = {}
_off = 0
for _name, _w in ([("qa", A_WIDTH), ("ka", A_KV_WIDTH), ("va", A_KV_WIDTH), ("za", A_WIDTH)]
                  + [(f"{p}b{i}", w) for i in range(3)
                     for p, w in (("q", B_WIDTH), ("k", B_KV_WIDTH), ("v", B_KV_WIDTH))]
                  + [("zb", B_WIDTH), ("ga", D_MODEL), ("gb", D_MODEL)]):
    _ORIG[_name] = (_off, _w)
    _off += _w
IN_WIDTH = _off

_PARTS = (("qa", "qb0", "qb1", "qb2", "kb0", "kb1", "kb2"),
          ("gb", "zb", "ka", "va", "vb0", "vb1"),
          ("ga", "za", "vb2"))
_ROTARY = {"qa", "qb0", "qb1", "qb2", "kb0", "kb1", "kb2", "ka"}
_PAIR = 2
PART_SLABS = IN_WIDTH // LANES // len(_PARTS)
_SEG = {}
_PART_COLS = []
for _part, _names in enumerate(_PARTS):
    _off, _cols = 0, []
    for _name in _names:
        _slabs = _ORIG[_name][1] // LANES
        assert _off % min(_slabs, 16) == 0
        _SEG[_name] = (_part, _off)
        _cols.append(np.arange(_ORIG[_name][0], _ORIG[_name][0] + _ORIG[_name][1]))
        _off += _slabs
    assert _off == PART_SLABS
    _PART_COLS.append(np.concatenate(_cols))
assert _SEG["va"] == (_SEG["ka"][0], _SEG["ka"][1] + 1) and _SEG["ka"][1] % 2 == 0
N_TILES = PART_SLABS // _PAIR
_TILE_COLS = _PAIR * LANES * len(_PARTS)
_COL_ORDER = np.concatenate([_PART_COLS[p][j * _PAIR * LANES:(j + 1) * _PAIR * LANES]
                             for j in range(N_TILES) for p in range(len(_PARTS))])


def _tile_rotary_flags(j):
    flags = []
    for part, names in enumerate(_PARTS):
        for t in range(_PAIR):
            slab = j * _PAIR + t
            name = next(n for n in names
                        if _SEG[n][1] <= slab < _SEG[n][1] + _ORIG[n][1] // LANES)
            flags.append(name in _ROTARY)
    return tuple(flags)


def _seg_spec(name, count, rows, row_block, **kwargs):
    part, first = _SEG[name]
    assert first % count == 0
    return pl.BlockSpec((None, count, rows, LANES),
                        lambda *g: (part, first // count, row_block(*g), 0), **kwargs)

_VMEM_LIMIT = 56 * 1024 * 1024


def _params(n_axes, vmem=_VMEM_LIMIT):
    return pltpu.CompilerParams(dimension_semantics=("arbitrary",) * n_axes,
                                vmem_limit_bytes=vmem)


def _rope_tables(pos):
    half = ROT_DIM // 2
    inv = ROPE_THETA ** (-jnp.arange(0, ROT_DIM, 2, dtype=jnp.float32) / ROT_DIM)
    dim = np.arange(LANES) % HEAD_DIM
    ang = pos[:, None] * inv[dim % half][None, :]
    cos, sin = jnp.cos(ang), jnp.sin(ang)
    first, second = (dim < half)[None, :], ((dim >= half) & (dim < ROT_DIM))[None, :]
    c = jnp.where(first | second, cos, 1.0)
    s1 = jnp.where(first, -sin, 0.0)
    s2 = jnp.where(second, sin, 0.0)
    return c, s1, s2


def _rope(h, c, s1, s2):
    half = ROT_DIM // 2
    return h * c + pltpu.roll(h, LANES - half, 1) * s1 + pltpu.roll(h, half, 1) * s2


def _to_half(piece, src_half, dst_half, lo):
    if src_half != dst_half:
        piece = pltpu.roll(piece, HALF, 1)
    keep = lo if dst_half == 0 else jnp.logical_not(lo)
    return jnp.where(keep, piece, 0.0)


def _merge_halves(even, odd, src_half, lo):
    if src_half == 1:
        even = pltpu.roll(even, HALF, 1)
    else:
        odd = pltpu.roll(odd, HALF, 1)
    return jnp.where(lo, even, odd)


def _nt(a, b):
    return lax.dot_general(a, b, (((1,), (1,)), ((), ())), preferred_element_type=jnp.float32)


def _band_bias(lo_off, hi_off, no_previous):
    i = lax.broadcasted_iota(jnp.int32, (BLK, 2 * BLK), 0)
    j = lax.broadcasted_iota(jnp.int32, (BLK, 2 * BLK), 1)
    band = (j >= i + lo_off) & (j <= i + hi_off)
    first_key = jnp.where(no_previous, BLK, 0)
    return (jnp.where(band, 0.0, -jnp.inf), jnp.where(band & (j >= first_key), 0.0, -jnp.inf))


def _softmax_pv(s, bias, v_win, sink=None):
    s = s + bias
    m = jnp.max(s, axis=1, keepdims=True)
    if sink is not None:
        m = jnp.maximum(m, sink)
    p = jnp.exp(s - m)
    den = jnp.sum(p, axis=1, keepdims=True)
    if sink is not None:
        den = den + jnp.exp(sink - m)
    o = jnp.dot(p.astype(jnp.bfloat16), v_win, preferred_element_type=jnp.float32)
    return o * (1.0 / den), m + jnp.log(den)


def _in_proj_kernel(x_ref, w_ref, b_ref, c_ref, s1_ref, s2_ref, o_ref, xb_ref):
    j = pl.program_id(1)

    @pl.when(j == 0)
    def _():
        xb_ref[...] = x_ref[...].astype(jnp.bfloat16)

    def tile(rotary):
        for part in range(len(_PARTS)):
            cols = slice(part * _PAIR * LANES, (part + 1) * _PAIR * LANES)
            acc = jnp.dot(xb_ref[...], w_ref[:, cols],
                          preferred_element_type=jnp.float32) + b_ref[:, cols]
            for t in range(_PAIR):
                col = acc[:, t * LANES:(t + 1) * LANES]
                if rotary[part * _PAIR + t]:
                    col = _rope(col, c_ref[...], s1_ref[...], s2_ref[...])
                o_ref[part, t] = col

    patterns = {}
    for tile_index in range(N_TILES):
        patterns.setdefault(_tile_rotary_flags(tile_index), []).append(tile_index)
    for rotary, tiles in patterns.items():
        hit = functools.reduce(jnp.logical_or, [j == t for t in tiles])
        pl.when(hit)(functools.partial(tile, rotary))


def _in_proj(x, w_bf, b, tabs, tm=1024):
    m, k = x.shape
    n_pos_blocks = tabs[0].shape[0] // tm
    tab = pl.BlockSpec((tm, LANES), lambda i, j: (i % n_pos_blocks, 0))
    return pl.pallas_call(
        _in_proj_kernel,
        grid=(m // tm, N_TILES),
        in_specs=[pl.BlockSpec((tm, k), lambda i, j: (i, 0)),
                  pl.BlockSpec((k, _TILE_COLS), lambda i, j: (0, j)),
                  pl.BlockSpec((1, _TILE_COLS), lambda i, j: (0, j)), tab, tab, tab],
        out_specs=pl.BlockSpec((len(_PARTS), _PAIR, tm, LANES), lambda i, j: (0, j, i, 0)),
        out_shape=jax.ShapeDtypeStruct((len(_PARTS), PART_SLABS, m, LANES), jnp.float32),
        scratch_shapes=[pltpu.VMEM((tm, k), jnp.bfloat16)],
        compiler_params=_params(2),
        name="in_proj",
    )(x, w_bf, b, *tabs)


_UNITS_PER_TRIP = 2


def _attend_pairs(slabs, k_wins, v_wins, kv_of_pair, bias, lo, sinks=None):
    result = []
    for p, slab in enumerate(slabs):
        col, half = kv_of_pair[p]
        slab = slab * SCALE
        outs, lses = [], []
        for sub in range(2):
            q_h = _to_half(slab, sub, half, lo).astype(jnp.bfloat16)
            o, lse = _softmax_pv(_nt(q_h, k_wins[col]), bias, v_wins[col],
                                 None if sinks is None else sinks[2 * p + sub])
            outs.append(o), lses.append(lse)
        result.append((_merge_halves(outs[0], outs[1], half, lo),
                       jnp.where(lo, jnp.broadcast_to(lses[0], (BLK, LANES)),
                                 jnp.broadcast_to(lses[1], (BLK, LANES)))))
    return result


def _attn_a_kernel(sink_ref, q_ref, kv_ref, kvp_ref, o_ref, k_buf, v_buf, *, n_sub):
    lo = lax.broadcasted_iota(jnp.int32, (BLK, LANES), 1) < HALF
    seq_start = pl.program_id(1) == 0
    group = A_Q_HEADS // A_KV_HEADS
    for buf, c in ((k_buf, 0), (v_buf, 1)):
        buf[0:BLK, :] = kvp_ref[c].astype(jnp.bfloat16)
        buf[BLK:, :] = kv_ref[c].astype(jnp.bfloat16)
    sinks = [sink_ref[h] for h in range(A_Q_HEADS)]
    kv_of_pair = [(0, (2 * p) // group) for p in range(A_Q_HEADS // 2)]

    def block(j, _):
        start = pl.multiple_of(j * BLK, BLK)
        rows = pl.ds(start, BLK)
        window = pl.ds(start, 2 * BLK)
        _, bias = _band_bias(1, A_WINDOW, seq_start & (j == 0))
        outs = _attend_pairs([q_ref[p, rows, :] for p in range(A_Q_HEADS // 2)],
                             [k_buf[window, :]], [v_buf[window, :]], kv_of_pair, bias, lo, sinks)
        for p, (o, _) in enumerate(outs):
            o_ref[p, rows, :] = o

    lax.fori_loop(0, n_sub, block, None)


def _attn_a(h3, sink, n, l, n_sub=4):
    rows = BLK * n_sub
    nsb = l // rows
    q_slabs = A_WIDTH // LANES
    tok = lambda i, sb: i * nsb + sb
    prev_blk = lambda i, sb: jnp.maximum(tok(i, sb) * n_sub - 1, 0)
    return pl.pallas_call(
        functools.partial(_attn_a_kernel, n_sub=n_sub),
        grid=(n, nsb),
        in_specs=[pl.BlockSpec(memory_space=pltpu.SMEM),
                  _seg_spec("qa", q_slabs, rows, tok),
                  _seg_spec("ka", 2, rows, tok),
                  _seg_spec("ka", 2, BLK, prev_blk)],
        out_specs=pl.BlockSpec((q_slabs, rows, LANES), lambda i, sb: (0, tok(i, sb), 0)),
        out_shape=jax.ShapeDtypeStruct((q_slabs, n * l, LANES), jnp.float32),
        scratch_shapes=[pltpu.VMEM((rows + BLK, LANES), jnp.bfloat16)] * 2,
        compiler_params=_params(2),
        name="attn_a",
    )(sink, h3, h3, h3)


def _attn_b_kernel(q_ref, k_ref, v_ref, kp_ref, vp_ref, o_ref, lse_ref, k_buf, v_buf, *,
                   dil, n_sub):
    n_col = B_KV_WIDTH // LANES
    lo = lax.broadcasted_iota(jnp.int32, (BLK, LANES), 1) < HALF
    seq_start = pl.program_id(1) == 0
    kv_of_pair = [(g // 2, g % 2) for g in range(B_KV_HEADS)]

    def fold(r, j):
        if dil == 1:
            return pl.ds(pl.multiple_of(j * BLK, BLK), BLK)
        return pl.ds(j * (BLK * dil) + r, BLK, stride=dil)

    def fill(r, _):
        for buf, prev, cur in ((k_buf, kp_ref, k_ref), (v_buf, vp_ref, v_ref)):
            for c in range(n_col):
                buf[r, c, 0:BLK, :] = prev[c, fold(r, 0), :].astype(jnp.bfloat16)
                for j in range(n_sub):
                    buf[r, c, (j + 1) * BLK:(j + 2) * BLK, :] = (
                        cur[c, fold(r, j), :].astype(jnp.bfloat16))

    def block(u, _):
        r, j = u // n_sub, u % n_sub
        window = pl.ds(pl.multiple_of(j * BLK, BLK), 2 * BLK)
        _, bias = _band_bias(0, BLK, seq_start & (j == 0))
        outs = _attend_pairs([q_ref[g, fold(r, j), :] for g in range(B_KV_HEADS)],
                             [k_buf[r, c, window, :] for c in range(n_col)],
                             [v_buf[r, c, window, :] for c in range(n_col)], kv_of_pair, bias, lo)
        for g, (o, lse) in enumerate(outs):
            o_ref[g, fold(r, j), :] = o
            lse_ref[g, fold(r, j), :] = lse

    if dil == 1:
        fill(0, None)
    else:
        lax.fori_loop(0, dil, fill, None)
    lax.fori_loop(0, dil * n_sub, block, None, unroll=_UNITS_PER_TRIP)


def _attn_b(h3, gi, dil, n, l, n_sub):
    prev_rows = BLK * dil
    rows = prev_rows * n_sub
    nsb = l // rows
    q_slabs = B_WIDTH // LANES
    kv_slabs = B_KV_WIDTH // LANES
    tok = lambda i, sb: i * nsb + sb
    prev_blk = lambda i, sb: jnp.maximum(tok(i, sb) * n_sub - 1, 0)
    out_blk = pl.BlockSpec((q_slabs, rows, LANES), lambda i, sb: (0, tok(i, sb), 0))

    prev_mode = dict(pipeline_mode=pl.Buffered(1)) if dil > 4 else {}

    def kv_specs(name):
        return (_seg_spec(f"{name}{gi}", kv_slabs, rows, tok),
                _seg_spec(f"{name}{gi}", kv_slabs, prev_rows, prev_blk, **prev_mode))

    (k_cur, k_prev), (v_cur, v_prev) = kv_specs("kb"), kv_specs("vb")
    return pl.pallas_call(
        functools.partial(_attn_b_kernel, dil=dil, n_sub=n_sub),
        grid=(n, nsb),
        in_specs=[_seg_spec(f"qb{gi}", q_slabs, rows, tok), k_cur, v_cur, k_prev, v_prev],
        out_specs=[out_blk, out_blk],
        out_shape=[jax.ShapeDtypeStruct((q_slabs, n * l, LANES), jnp.float32)] * 2,
        scratch_shapes=[pltpu.VMEM((dil, kv_slabs, (n_sub + 1) * BLK, LANES), jnp.bfloat16)] * 2,
        compiler_params=_params(2),
        name=f"attn_b{gi}",
    )(h3, h3, h3, h3, h3)


def _kv_state_kernel(k_ref, v_ref, o_ref):
    n_k = k_ref.shape[0]
    for c in range(n_k):
        o_ref[0, c * LANES:(c + 1) * LANES, :] = jnp.transpose(k_ref[c])
        o_ref[0, (n_k + c) * LANES:(n_k + c + 1) * LANES, :] = jnp.transpose(v_ref[c])


def _kv_state(h3, k_name, v_name, width, win, n, l, name):
    slabs = width // LANES
    chunk = min(win, 4 * BLK)
    per_seq = win // chunk
    first_chunk = lambda i, t: ((i + 1) * l - win) // chunk + t
    spec = lambda s: _seg_spec(s, slabs, chunk, first_chunk)
    return pl.pallas_call(
        _kv_state_kernel,
        grid=(n, per_seq),
        in_specs=[spec(k_name), spec(v_name)],
        out_specs=pl.BlockSpec((1, 2 * width, chunk), lambda i, t: (i, 0, t)),
        out_shape=jax.ShapeDtypeStruct((n, 2 * width, win), jnp.float32),
        compiler_params=_params(2),
        name=name,
    )(h3, h3)


_PAD = 16
_SAMPLE_SEQS_PER_STEP = 2


def _pad_rows(x, rows=_PAD, at=0):
    parts = []
    if at:
        parts.append(jnp.zeros((at, x.shape[1]), x.dtype))
    parts.append(x)
    if rows - at - x.shape[0]:
        parts.append(jnp.zeros((rows - at - x.shape[0], x.shape[1]), x.dtype))
    return jnp.concatenate(parts, axis=0)


def _sample_bias(shape, first_key, cb, dil, max_dist):
    t = lax.broadcasted_iota(jnp.int32, shape, 0) & (DEC_SEQ - 1)
    dist = cb + t - (lax.broadcasted_iota(jnp.int32, shape, 1) + first_key)
    valid = (dist >= 0) & (dist <= max_dist) & ((dist & (dil - 1)) == 0)
    return jnp.where(valid, 0.0, -jnp.inf)


def _sample_attend(qz, cache_ref, b, bias_ref, k_new, v_new, width, dil, max_dist, sink=None):
    cb = cache_ref.shape[2]
    qz = qz.astype(jnp.bfloat16)
    s_c = jnp.dot(qz, cache_ref[b, 0:width, :].astype(jnp.bfloat16),
                  preferred_element_type=jnp.float32) + bias_ref[...]
    s_n = _nt(qz, _pad_rows(k_new).astype(jnp.bfloat16))
    s_n = s_n + _sample_bias(s_n.shape, cb, cb, dil, max_dist)
    m = jnp.maximum(jnp.max(s_c, axis=1, keepdims=True), jnp.max(s_n, axis=1, keepdims=True))
    if sink is not None:
        m = jnp.maximum(m, sink)
    p_c, p_n = jnp.exp(s_c - m), jnp.exp(s_n - m)
    den = jnp.sum(p_c, axis=1, keepdims=True) + jnp.sum(p_n, axis=1, keepdims=True)
    if sink is not None:
        den = den + jnp.exp(sink - m)
    o = (_nt(p_c.astype(jnp.bfloat16), cache_ref[b, width:2 * width, :].astype(jnp.bfloat16))
         + jnp.dot(p_n.astype(jnp.bfloat16), _pad_rows(v_new).astype(jnp.bfloat16),
                   preferred_element_type=jnp.float32))
    return o / den, m + jnp.log(den)


def _shift_cache(cache_ref, new_ref, b, new_rows):
    cb = cache_ref.shape[2]
    lane = lax.broadcasted_iota(jnp.int32, (cache_ref.shape[1], LANES), 1)
    keep = lane < LANES - DEC_SEQ
    tail = jnp.transpose(_pad_rows(new_rows, LANES, LANES - DEC_SEQ))
    cur = pltpu.roll(cache_ref[b, :, 0:LANES], LANES - DEC_SEQ, 1)
    for j in range(cb // LANES):
        if j + 1 < cb // LANES:
            nxt = pltpu.roll(cache_ref[b, :, (j + 1) * LANES:(j + 2) * LANES], LANES - DEC_SEQ, 1)
        else:
            nxt = tail
        new_ref[b, :, j * LANES:(j + 1) * LANES] = jnp.where(keep, cur, nxt)
        cur = nxt


def _sample_kernel(*refs):
    cache_refs, bias_refs = refs[1:5], refs[-4:]

    @pl.when(pl.program_id(0) == 0)
    def _():
        windows = [(1, A_WINDOW - 1)] + [(dil, win) for win, dil in B_PAIRS]
        for cache, bias, (dil, max_dist) in zip(cache_refs, bias_refs, windows):
            bias[...] = _sample_bias(bias.shape, 0, cache.shape[2], dil, max_dist)

    for b in range(_SAMPLE_SEQS_PER_STEP):
        _sample_one(b, *refs)


def _sample_one(b, h_ref, ca_ref, c0_ref, c1_ref, c2_ref, sink_ref,
                oa_ref, ob_ref, na_ref, n0_ref, n1_ref, n2_ref, ba_ref, b0_ref, b1_ref, b2_ref):
    lo = lax.broadcasted_iota(jnp.int32, (DEC_SEQ, LANES), 1) < HALF
    rows = slice(b * DEC_SEQ, (b + 1) * DEC_SEQ)
    slabs = lambda name, count: [h_ref[_SEG[name][0], _SEG[name][1] + c, rows, :]
                                 for c in range(count)]

    group = A_Q_HEADS // A_KV_HEADS
    pieces = []
    for pair, slab in enumerate(slabs("qa", A_WIDTH // LANES)):
        for sub in range(2):
            pieces.append(_to_half(slab * SCALE, sub, (2 * pair + sub) // group, lo))
    (k_new,), (v_new,) = slabs("ka", 1), slabs("va", 1)
    o, _ = _sample_attend(jnp.concatenate(pieces, axis=0), ca_ref, b, ba_ref, k_new, v_new,
                          A_KV_WIDTH, 1, A_WINDOW - 1, sink_ref[...])
    for pair in range(A_Q_HEADS // 2):
        oa_ref[pair, rows, :] = _merge_halves(o[(2 * pair) * DEC_SEQ:(2 * pair + 1) * DEC_SEQ, :],
                                     o[(2 * pair + 1) * DEC_SEQ:(2 * pair + 2) * DEC_SEQ, :],
                                     (2 * pair) // group, lo)
    _shift_cache(ca_ref, na_ref, b, jnp.concatenate([k_new, v_new], axis=1))

    n_col = B_KV_WIDTH // LANES
    zeros = jnp.zeros((DEC_SEQ, LANES), jnp.float32)
    outs, lses = [], []
    for gi, ((win, dil), c_ref, n_ref, bias_ref) in enumerate(
            zip(B_PAIRS, (c0_ref, c1_ref, c2_ref), (n0_ref, n1_ref, n2_ref),
                (b0_ref, b1_ref, b2_ref))):
        pieces = []
        for g, slab in enumerate(slabs(f"qb{gi}", B_WIDTH // LANES)):
            for sub in range(2):
                cols = [zeros] * n_col
                cols[g // 2] = _to_half(slab * SCALE, sub, g % 2, lo)
                pieces.append(jnp.concatenate(cols, axis=1))
        k_new = jnp.concatenate(slabs(f"kb{gi}", n_col), axis=1)
        v_new = jnp.concatenate(slabs(f"vb{gi}", n_col), axis=1)
        o, lse = _sample_attend(jnp.concatenate(pieces, axis=0), c_ref, b, bias_ref, k_new, v_new,
                                B_KV_WIDTH, dil, win)
        outs.append(o), lses.append(lse)
        _shift_cache(c_ref, n_ref, b, jnp.concatenate([k_new, v_new], axis=1))
    top = functools.reduce(jnp.maximum, lses)
    es = [jnp.exp(v - top) for v in lses]
    tot = functools.reduce(lambda a, c: a + c, es)
    o = functools.reduce(lambda a, c: a + c, [(e / tot) * v for e, v in zip(es, outs)])
    for g in range(B_KV_HEADS):
        col = slice((g // 2) * LANES, (g // 2 + 1) * LANES)
        ob_ref[g, rows, :] = _merge_halves(o[(2 * g) * DEC_SEQ:(2 * g + 1) * DEC_SEQ, col],
                                  o[(2 * g + 1) * DEC_SEQ:(2 * g + 2) * DEC_SEQ, col], g % 2, lo)


def _sample_step(hs3, caches_t, sink_rows):
    n = caches_t[0].shape[0]
    per_step = _SAMPLE_SEQS_PER_STEP
    cache_specs = [pl.BlockSpec((per_step,) + c.shape[1:], lambda i: (i, 0, 0)) for c in caches_t]
    row_blk = lambda slabs: pl.BlockSpec((slabs, per_step * DEC_SEQ, LANES), lambda i: (0, i, 0))
    return pl.pallas_call(
        _sample_kernel,
        grid=(n // per_step,),
        in_specs=[pl.BlockSpec((len(_PARTS), PART_SLABS, per_step * DEC_SEQ, LANES),
                               lambda i: (0, 0, i, 0))] + cache_specs
        + [pl.BlockSpec((A_Q_HEADS * DEC_SEQ, 1), lambda i: (0, 0))],
        out_specs=[row_blk(A_WIDTH // LANES), row_blk(B_WIDTH // LANES)] + cache_specs,
        out_shape=[jax.ShapeDtypeStruct((A_WIDTH // LANES, n * DEC_SEQ, LANES), jnp.float32),
                   jax.ShapeDtypeStruct((B_WIDTH // LANES, n * DEC_SEQ, LANES), jnp.float32)]
        + [jax.ShapeDtypeStruct(c.shape, jnp.float32) for c in caches_t],
        scratch_shapes=[pltpu.VMEM((rows * DEC_SEQ, c.shape[2]), jnp.float32)
                        for rows, c in zip([A_Q_HEADS] + [B_Q_HEADS] * 3, caches_t)],
        compiler_params=_params(1),
        name="sample_step",
    )(hs3, *caches_t, sink_rows)


def _sigmoid(x):
    return 1.0 / (1.0 + jnp.exp(-x))


def _out_kernel(*refs, n_groups):
    x_ref, oa_ref, za_ref, ga_ref, gb_ref, zb_ref = refs[:6]
    ob_refs = refs[6:6 + n_groups]
    lse_refs = refs[6 + n_groups:6 + 2 * n_groups] if n_groups > 1 else ()
    wa_ref, wb_ref, wo_ref, lng_ref, lnb_ref, y_ref = refs[-6:]

    def gated(o, z):
        return (o * (z * _sigmoid(z))).astype(jnp.bfloat16)

    ya = jnp.concatenate([gated(oa_ref[s], za_ref[s]) for s in range(A_WIDTH // LANES)], axis=1)
    yb = []
    for s in range(B_WIDTH // LANES):
        if n_groups > 1:
            lses = [r[s] for r in lse_refs]
            top = functools.reduce(jnp.maximum, lses)
            es = [jnp.exp(v - top) for v in lses]
            tot = functools.reduce(lambda a, c: a + c, es)
            ob = functools.reduce(lambda a, c: a + c,
                                  [(e / tot) * r[s] for e, r in zip(es, ob_refs)])
        else:
            ob = ob_refs[0][s]
        yb.append(gated(ob, zb_ref[s]))
    yb = jnp.concatenate(yb, axis=1)
    da = jnp.dot(ya, wa_ref[...], preferred_element_type=jnp.float32)
    db = jnp.dot(yb, wb_ref[...], preferred_element_type=jnp.float32)
    m = jnp.concatenate(
        [(_sigmoid(ga_ref[s]) * da[:, s * LANES:(s + 1) * LANES]
          + _sigmoid(gb_ref[s]) * db[:, s * LANES:(s + 1) * LANES]).astype(jnp.bfloat16)
         for s in range(D_MODEL // LANES)], axis=1)
    h = DN_ALPHA * x_ref[...] + jnp.dot(m, wo_ref[...], preferred_element_type=jnp.float32)
    mu = jnp.mean(h, axis=-1, keepdims=True)
    var = jnp.mean(jnp.square(h - mu), axis=-1, keepdims=True)
    y_ref[...] = (h - mu) * lax.rsqrt(var + LN_EPS) * lng_ref[...] + lnb_ref[...]


def _out_proj(x, h3, oa, obs, lses, wa, wb, wo, ln_g, ln_b, tm=256):
    m = x.shape[0]
    n_groups = len(obs)

    mixer_out = lambda count: pl.BlockSpec((count, tm, LANES), lambda i: (0, i, 0))
    seg = lambda name, count: _seg_spec(name, count, tm, lambda i: i)
    row = pl.BlockSpec((tm, D_MODEL), lambda i: (i, 0))
    whole = lambda a: pl.BlockSpec(a.shape, lambda i: (0, 0), pipeline_mode=pl.Buffered(1))
    a_slabs, b_slabs, d_slabs = A_WIDTH // LANES, B_WIDTH // LANES, D_MODEL // LANES
    in_specs = ([row, mixer_out(a_slabs), seg("za", a_slabs), seg("ga", d_slabs),
                 seg("gb", d_slabs), seg("zb", b_slabs)]
                + [mixer_out(b_slabs)] * (len(obs) + len(lses))
                + [whole(wa), whole(wb), whole(wo), whole(ln_g), whole(ln_b)])
    return pl.pallas_call(
        functools.partial(_out_kernel, n_groups=n_groups),
        grid=(m // tm,),
        in_specs=in_specs,
        out_specs=row,
        out_shape=jax.ShapeDtypeStruct((m, D_MODEL), jnp.float32),
        compiler_params=_params(1),
        name=f"out_proj{n_groups}",
    )(x, oa, h3, h3, h3, h3, *obs, *lses, wa, wb, wo, ln_g, ln_b)


_UNIT = _PAIR * LANES
assert all(_COL_ORDER[u * _UNIT] % _UNIT == 0
           and np.all(np.diff(_COL_ORDER[u * _UNIT:(u + 1) * _UNIT]) == 1)
           for u in range(IN_WIDTH // _UNIT))


def _cast_permute_kernel(order_ref, w_ref, o_ref):
    del order_ref
    o_ref[...] = w_ref[...].astype(jnp.bfloat16)


def _cast_permute_w(w):
    k, n = w.shape
    unit_order = jnp.asarray(_COL_ORDER[::_UNIT] // _UNIT, jnp.int32)
    return pl.pallas_call(
        _cast_permute_kernel,
        grid_spec=pltpu.PrefetchScalarGridSpec(
            num_scalar_prefetch=1, grid=(n // _UNIT,),
            in_specs=[pl.BlockSpec((k, _UNIT), lambda j, order: (0, order[j]))],
            out_specs=pl.BlockSpec((k, _UNIT), lambda j, order: (0, j))),
        out_shape=jax.ShapeDtypeStruct((k, n), jnp.bfloat16),
        compiler_params=_params(1),
        name="cast_permute_w",
    )(unit_order, w)


def kernel(x_prompt, x_sample, cache_a_kv, cache_b0_kv, cache_b1_kv, cache_b2_kv,
           w_in, b_in, sink_a, w_br_a, w_br_b, w_out, ln_g, ln_b):
    n, l, _ = x_prompt.shape
    ns, t, _ = x_sample.shape
    assert w_in.shape[0] == 1 and t == DEC_SEQ and l == SEQ
    assert PAST_LEN >= max(w for w, _ in B_PAIRS)
    w_bf = _cast_permute_w(w_in[0])
    b_p = jnp.take(b_in[0], jnp.asarray(_COL_ORDER, jnp.int32))[None, :]
    wa, wb, wo = (w[0].astype(jnp.bfloat16) for w in (w_br_a, w_br_b, w_out))
    tabs_p = _rope_tables(jnp.arange(l, dtype=jnp.float32))
    tabs_s = [jnp.tile(tb, (ns, 1)) for tb in
              _rope_tables(PAST_LEN + jnp.arange(t, dtype=jnp.float32))]
    sink = sink_a[0]
    sink_rows = jnp.repeat(sink, DEC_SEQ)[:, None]

    xp = x_prompt.reshape(n * l, D_MODEL)
    hp = _in_proj(xp, w_bf, b_p, tabs_p)
    oa = _attn_a(hp, sink, n, l)
    st_a = _kv_state(hp, "ka", "va", A_KV_WIDTH, min(A_WINDOW, l), n, l, "state_a")
    obs, lses, st_b = [], [], []
    for gi, (win, dil) in enumerate(B_PAIRS):
        o, lse = _attn_b(hp, gi, dil, n, l, n_sub={1: 4, 4: 2, 16: 1}[dil])
        obs.append(o), lses.append(lse)
        st_b.append(_kv_state(hp, f"kb{gi}", f"vb{gi}", B_KV_WIDTH, min(win, l), n, l,
                              f"state_b{gi}"))
    yp = _out_proj(xp, hp, oa, obs, lses, wa, wb, wo, ln_g, ln_b)

    xs = x_sample.reshape(ns * t, D_MODEL)
    hs = _in_proj(xs, w_bf, b_p, tabs_s)
    to_t = lambda c, w: jnp.transpose(c[0].reshape(ns, -1, w), (0, 2, 1))
    caches_t = [to_t(cache_a_kv, 2 * A_KV_WIDTH)] + [
        to_t(c, 2 * B_KV_WIDTH) for c in (cache_b0_kv, cache_b1_kv, cache_b2_kv)]
    oa_s, ob_s, *new_t = _sample_step(hs, caches_t, sink_rows)
    ys = _out_proj(xs, hs, oa_s, [ob_s], [], wa, wb, wo, ln_g, ln_b)

    kv_shape = lambda a, heads: a.reshape((1, a.shape[0], a.shape[1], 2, heads, HEAD_DIM))
    from_t = lambda a, heads: kv_shape(jnp.transpose(a, (0, 2, 1)), heads)
    return (yp.reshape(n, l, D_MODEL), ys.reshape(ns, t, D_MODEL),
            from_t(st_a, A_KV_HEADS), *[from_t(s, B_KV_HEADS) for s in st_b],
            from_t(new_t[0], A_KV_HEADS), *[from_t(c, B_KV_HEADS) for c in new_t[1:]])
```

```python
import functools

import jax
import jax.numpy as jnp
import numpy as np
from jax import lax
from jax.experimental import pallas as pl
from jax.experimental.pallas import tpu as pltpu

D_MODEL = 2048
SEQ = 8192
DEC_SEQ = 8
PAST_LEN = 16384
HEAD_DIM = 64
ROT_DIM = HEAD_DIM // 4
ROPE_THETA = 500000.0
BLK = 128
A_WINDOW = 128
A_Q_HEADS = 16
A_KV_HEADS = 2
B_PAIRS = ((128, 1), (512, 4), (2048, 16))
B_Q_HEADS = 8
B_KV_HEADS = 4
A_WIDTH = A_Q_HEADS * HEAD_DIM
B_WIDTH = B_Q_HEADS * HEAD_DIM
A_KV_WIDTH = A_KV_HEADS * HEAD_DIM
B_KV_WIDTH = B_KV_HEADS * HEAD_DIM
SCALE = HEAD_DIM ** -0.5
DN_ALPHA = 2.0 ** 0.25
LN_EPS = 1e-5
LANES = 128
HALF = LANES // 2

_ORIG = {}
_off = 0
for _name, _w in ([("qa", A_WIDTH), ("ka", A_KV_WIDTH), ("va", A_KV_WIDTH), ("za", A_WIDTH)]
                  + [(f"{p}b{i}", w) for i in range(3)
                     for p, w in (("q", B_WIDTH), ("k", B_KV_WIDTH), ("v", B_KV_WIDTH))]
                  + [("zb", B_WIDTH), ("ga", D_MODEL), ("gb", D_MODEL)]):
    _ORIG[_name] = (_off, _w)
    _off += _w
IN_WIDTH = _off

_PARTS = (("qa", "qb0", "qb1", "qb2", "kb0", "kb1", "kb2"),
          ("gb", "zb", "ka", "va", "vb0", "vb1"),
          ("ga", "za", "vb2"))
_ROTARY = {"qa", "qb0", "qb1", "qb2", "kb0", "kb1", "kb2", "ka"}
_PAIR = 2
PART_SLABS = IN_WIDTH // LANES // len(_PARTS)
_SEG = {}
_PART_COLS = []
for _part, _names in enumerate(_PARTS):
    _off, _cols = 0, []
    for _name in _names:
        _slabs = _ORIG[_name][1] // LANES
        assert _off % min(_slabs, 16) == 0
        _SEG[_name] = (_part, _off)
        _cols.append(np.arange(_ORIG[_name][0], _ORIG[_name][0] + _ORIG[_name][1]))
        _off += _slabs
    assert _off == PART_SLABS
    _PART_COLS.append(np.concatenate(_cols))
assert _SEG["va"] == (_SEG["ka"][0], _SEG["ka"][1] + 1) and _SEG["ka"][1] % 2 == 0
N_TILES = PART_SLABS // _PAIR
_TILE_COLS = _PAIR * LANES * len(_PARTS)
_COL_ORDER = np.concatenate([_PART_COLS[p][j * _PAIR * LANES:(j + 1) * _PAIR * LANES]
                             for j in range(N_TILES) for p in range(len(_PARTS))])


def _tile_rotary_flags(j):
    flags = []
    for part, names in enumerate(_PARTS):
        for t in range(_PAIR):
            slab = j * _PAIR + t
            name = next(n for n in names
                        if _SEG[n][1] <= slab < _SEG[n][1] + _ORIG[n][1] // LANES)
            flags.append(name in _ROTARY)
    return tuple(flags)


def _seg_spec(name, count, rows, row_block, **kwargs):
    part, first = _SEG[name]
    assert first % count == 0
    return pl.BlockSpec((None, count, rows, LANES),
                        lambda *g: (part, first // count, row_block(*g), 0), **kwargs)

_VMEM_LIMIT = 56 * 1024 * 1024


def _params(n_axes, vmem=_VMEM_LIMIT):
    return pltpu.CompilerParams(dimension_semantics=("arbitrary",) * n_axes,
                                vmem_limit_bytes=vmem)


def _rope_tables(pos):
    half = ROT_DIM // 2
    inv = ROPE_THETA ** (-jnp.arange(0, ROT_DIM, 2, dtype=jnp.float32) / ROT_DIM)
    dim = np.arange(LANES) % HEAD_DIM
    ang = pos[:, None] * inv[dim % half][None, :]
    cos, sin = jnp.cos(ang), jnp.sin(ang)
    first, second = (dim < half)[None, :], ((dim >= half) & (dim < ROT_DIM))[None, :]
    c = jnp.where(first | second, cos, 1.0)
    s1 = jnp.where(first, -sin, 0.0)
    s2 = jnp.where(second, sin, 0.0)
    return c, s1, s2


def _rope(h, c, s1, s2):
    half = ROT_DIM // 2
    return h * c + pltpu.roll(h, LANES - half, 1) * s1 + pltpu.roll(h, half, 1) * s2


def _to_half(piece, src_half, dst_half, lo):
    if src_half != dst_half:
        piece = pltpu.roll(piece, HALF, 1)
    keep = lo if dst_half == 0 else jnp.logical_not(lo)
    return jnp.where(keep, piece, 0.0)


def _merge_halves(even, odd, src_half, lo):
    if src_half == 1:
        even = pltpu.roll(even, HALF, 1)
    else:
        odd = pltpu.roll(odd, HALF, 1)
    return jnp.where(lo, even, odd)


def _nt(a, b):
    return lax.dot_general(a, b, (((1,), (1,)), ((), ())), preferred_element_type=jnp.float32)


def _band_bias(lo_off, hi_off, no_previous):
    i = lax.broadcasted_iota(jnp.int32, (BLK, 2 * BLK), 0)
    j = lax.broadcasted_iota(jnp.int32, (BLK, 2 * BLK), 1)
    band = (j >= i + lo_off) & (j <= i + hi_off)
    first_key = jnp.where(no_previous, BLK, 0)
    return (jnp.where(band, 0.0, -jnp.inf), jnp.where(band & (j >= first_key), 0.0, -jnp.inf))


def _softmax_pv(s, bias, v_win, sink=None):
    s = s + bias
    m = jnp.max(s, axis=1, keepdims=True)
    if sink is not None:
        m = jnp.maximum(m, sink)
    p = jnp.exp(s - m)
    den = jnp.sum(p, axis=1, keepdims=True)
    if sink is not None:
        den = den + jnp.exp(sink - m)
    o = jnp.dot(p.astype(jnp.bfloat16), v_win, preferred_element_type=jnp.float32)
    return o * (1.0 / den), m + jnp.log(den)


def _in_proj_kernel(x_ref, w_ref, b_ref, c_ref, s1_ref, s2_ref, o_ref, xb_ref):
    j = pl.program_id(1)

    @pl.when(j == 0)
    def _():
        xb_ref[...] = x_ref[...].astype(jnp.bfloat16)

    def tile(rotary):
        for part in range(len(_PARTS)):
            cols = slice(part * _PAIR * LANES, (part + 1) * _PAIR * LANES)
            acc = jnp.dot(xb_ref[...], w_ref[:, cols],
                          preferred_element_type=jnp.float32) + b_ref[:, cols]
            for t in range(_PAIR):
                col = acc[:, t * LANES:(t + 1) * LANES]
                if rotary[part * _PAIR + t]:
                    col = _rope(col, c_ref[...], s1_ref[...], s2_ref[...])
                o_ref[part, t] = col

    patterns = {}
    for tile_index in range(N_TILES):
        patterns.setdefault(_tile_rotary_flags(tile_index), []).append(tile_index)
    for rotary, tiles in patterns.items():
        hit = functools.reduce(jnp.logical_or, [j == t for t in tiles])
        pl.when(hit)(functools.partial(tile, rotary))


def _in_proj(x, w_bf, b, tabs, tm):
    m, k = x.shape
    n_pos_blocks = tabs[0].shape[0] // tm
    tab = pl.BlockSpec((tm, LANES), lambda i, j: (i % n_pos_blocks, 0))
    x_mode = dict(pipeline_mode=pl.Buffered(1)) if tm > 1024 else {}
    return pl.pallas_call(
        _in_proj_kernel,
        grid=(m // tm, N_TILES),
        in_specs=[pl.BlockSpec((tm, k), lambda i, j: (i, 0), **x_mode),
                  pl.BlockSpec((k, _TILE_COLS), lambda i, j: (0, j)),
                  pl.BlockSpec((1, _TILE_COLS), lambda i, j: (0, j)), tab, tab, tab],
        out_specs=pl.BlockSpec((len(_PARTS), _PAIR, tm, LANES), lambda i, j: (0, j, i, 0)),
        out_shape=jax.ShapeDtypeStruct((len(_PARTS), PART_SLABS, m, LANES), jnp.float32),
        scratch_shapes=[pltpu.VMEM((tm, k), jnp.bfloat16)],
        compiler_params=_params(2),
        name="in_proj",
    )(x, w_bf, b, *tabs)


_UNITS_PER_TRIP = 2


def _attend_pairs(slabs, k_wins, v_wins, kv_of_pair, bias, lo, sinks=None):
    result = []
    for p, slab in enumerate(slabs):
        col, half = kv_of_pair[p]
        slab = slab * SCALE
        outs, lses = [], []
        for sub in range(2):
            q_h = _to_half(slab, sub, half, lo).astype(jnp.bfloat16)
            o, lse = _softmax_pv(_nt(q_h, k_wins[col]), bias, v_wins[col],
                                 None if sinks is None else sinks[2 * p + sub])
            outs.append(o), lses.append(lse)
        result.append((_merge_halves(outs[0], outs[1], half, lo),
                       jnp.where(lo, jnp.broadcast_to(lses[0], (BLK, LANES)),
                                 jnp.broadcast_to(lses[1], (BLK, LANES)))))
    return result


def _attn_a_kernel(sink_ref, q_ref, kv_ref, kvp_ref, o_ref, k_buf, v_buf, *, n_sub):
    lo = lax.broadcasted_iota(jnp.int32, (BLK, LANES), 1) < HALF
    seq_start = pl.program_id(1) == 0
    group = A_Q_HEADS // A_KV_HEADS
    for buf, c in ((k_buf, 0), (v_buf, 1)):
        buf[0:BLK, :] = kvp_ref[c].astype(jnp.bfloat16)
        buf[BLK:, :] = kv_ref[c].astype(jnp.bfloat16)
    sinks = [sink_ref[h] for h in range(A_Q_HEADS)]
    kv_of_pair = [(0, (2 * p) // group) for p in range(A_Q_HEADS // 2)]

    def block(j, _):
        start = pl.multiple_of(j * BLK, BLK)
        rows = pl.ds(start, BLK)
        window = pl.ds(start, 2 * BLK)
        _, bias = _band_bias(1, A_WINDOW, seq_start & (j == 0))
        outs = _attend_pairs([q_ref[p, rows, :] for p in range(A_Q_HEADS // 2)],
                             [k_buf[window, :]], [v_buf[window, :]], kv_of_pair, bias, lo, sinks)
        for p, (o, _) in enumerate(outs):
            o_ref[p, rows, :] = o

    lax.fori_loop(0, n_sub, block, None)


def _attn_a(h3, sink, n, l, n_sub=4):
    rows = BLK * n_sub
    nsb = l // rows
    q_slabs = A_WIDTH // LANES
    tok = lambda i, sb: i * nsb + sb
    prev_blk = lambda i, sb: jnp.maximum(tok(i, sb) * n_sub - 1, 0)
    return pl.pallas_call(
        functools.partial(_attn_a_kernel, n_sub=n_sub),
        grid=(n, nsb),
        in_specs=[pl.BlockSpec(memory_space=pltpu.SMEM),
                  _seg_spec("qa", q_slabs, rows, tok),
                  _seg_spec("ka", 2, rows, tok),
                  _seg_spec("ka", 2, BLK, prev_blk)],
        out_specs=pl.BlockSpec((q_slabs, rows, LANES), lambda i, sb: (0, tok(i, sb), 0)),
        out_shape=jax.ShapeDtypeStruct((q_slabs, n * l, LANES), jnp.float32),
        scratch_shapes=[pltpu.VMEM((rows + BLK, LANES), jnp.bfloat16)] * 2,
        compiler_params=_params(2),
        name="attn_a",
    )(sink, h3, h3, h3)


def _attn_b_kernel(q_ref, k_ref, v_ref, kp_ref, vp_ref, o_ref, lse_ref, k_buf, v_buf, *,
                   dil, n_sub):
    n_col = B_KV_WIDTH // LANES
    lo = lax.broadcasted_iota(jnp.int32, (BLK, LANES), 1) < HALF
    seq_start = pl.program_id(1) == 0
    kv_of_pair = [(g // 2, g % 2) for g in range(B_KV_HEADS)]

    def fold(r, j):
        if dil == 1:
            return pl.ds(pl.multiple_of(j * BLK, BLK), BLK)
        return pl.ds(j * (BLK * dil) + r, BLK, stride=dil)

    def fill(r, _):
        for buf, prev, cur in ((k_buf, kp_ref, k_ref), (v_buf, vp_ref, v_ref)):
            for c in range(n_col):
                buf[r, c, 0:BLK, :] = prev[c, fold(r, 0), :].astype(jnp.bfloat16)
                for j in range(n_sub):
                    buf[r, c, (j + 1) * BLK:(j + 2) * BLK, :] = (
                        cur[c, fold(r, j), :].astype(jnp.bfloat16))

    def block(u, _):
        r, j = u // n_sub, u % n_sub
        window = pl.ds(pl.multiple_of(j * BLK, BLK), 2 * BLK)
        _, bias = _band_bias(0, BLK, seq_start & (j == 0))
        outs = _attend_pairs([q_ref[g, fold(r, j), :] for g in range(B_KV_HEADS)],
                             [k_buf[r, c, window, :] for c in range(n_col)],
                             [v_buf[r, c, window, :] for c in range(n_col)], kv_of_pair, bias, lo)
        for g, (o, lse) in enumerate(outs):
            o_ref[g, fold(r, j), :] = o
            lse_ref[g, fold(r, j), :] = lse

    if dil == 1:
        fill(0, None)
    else:
        lax.fori_loop(0, dil, fill, None)
    lax.fori_loop(0, dil * n_sub, block, None, unroll=_UNITS_PER_TRIP)


def _attn_b(h3, gi, dil, n, l, n_sub):
    prev_rows = BLK * dil
    rows = prev_rows * n_sub
    nsb = l // rows
    q_slabs = B_WIDTH // LANES
    kv_slabs = B_KV_WIDTH // LANES
    tok = lambda i, sb: i * nsb + sb
    prev_blk = lambda i, sb: jnp.maximum(tok(i, sb) * n_sub - 1, 0)
    out_blk = pl.BlockSpec((q_slabs, rows, LANES), lambda i, sb: (0, tok(i, sb), 0))

    def kv_specs(name):
        return (_seg_spec(f"{name}{gi}", kv_slabs, rows, tok),
                _seg_spec(f"{name}{gi}", kv_slabs, prev_rows, prev_blk))

    (k_cur, k_prev), (v_cur, v_prev) = kv_specs("kb"), kv_specs("vb")
    return pl.pallas_call(
        functools.partial(_attn_b_kernel, dil=dil, n_sub=n_sub),
        grid=(n, nsb),
        in_specs=[_seg_spec(f"qb{gi}", q_slabs, rows, tok), k_cur, v_cur, k_prev, v_prev],
        out_specs=[out_blk, out_blk],
        out_shape=[jax.ShapeDtypeStruct((q_slabs, n * l, LANES), jnp.float32)] * 2,
        scratch_shapes=[pltpu.VMEM((dil, kv_slabs, (n_sub + 1) * BLK, LANES), jnp.bfloat16)] * 2,
        compiler_params=_params(2),
        name=f"attn_b{gi}",
    )(h3, h3, h3, h3, h3)


def _kv_state_kernel(k_ref, v_ref, o_ref):
    n_k = k_ref.shape[0]
    for c in range(n_k):
        o_ref[0, c * LANES:(c + 1) * LANES, :] = jnp.transpose(k_ref[c])
        o_ref[0, (n_k + c) * LANES:(n_k + c + 1) * LANES, :] = jnp.transpose(v_ref[c])


def _kv_state(h3, k_name, v_name, width, win, n, l, name):
    slabs = width // LANES
    chunk = min(win, 4 * BLK)
    per_seq = win // chunk
    first_chunk = lambda i, t: ((i + 1) * l - win) // chunk + t
    spec = lambda s: _seg_spec(s, slabs, chunk, first_chunk)
    return pl.pallas_call(
        _kv_state_kernel,
        grid=(n, per_seq),
        in_specs=[spec(k_name), spec(v_name)],
        out_specs=pl.BlockSpec((1, 2 * width, chunk), lambda i, t: (i, 0, t)),
        out_shape=jax.ShapeDtypeStruct((n, 2 * width, win), jnp.float32),
        compiler_params=_params(2),
        name=name,
    )(h3, h3)


_PAD = 16
_SAMPLE_SEQS_PER_STEP = 2


def _pad_rows(x, rows=_PAD, at=0):
    parts = []
    if at:
        parts.append(jnp.zeros((at, x.shape[1]), x.dtype))
    parts.append(x)
    if rows - at - x.shape[0]:
        parts.append(jnp.zeros((rows - at - x.shape[0], x.shape[1]), x.dtype))
    return jnp.concatenate(parts, axis=0)


def _sample_bias(shape, first_key, cb, dil, max_dist):
    t = lax.broadcasted_iota(jnp.int32, shape, 0) & (DEC_SEQ - 1)
    dist = cb + t - (lax.broadcasted_iota(jnp.int32, shape, 1) + first_key)
    valid = (dist >= 0) & (dist <= max_dist) & ((dist & (dil - 1)) == 0)
    return jnp.where(valid, 0.0, -jnp.inf)


def _sample_attend(qz, cache_ref, b, bias_ref, k_new, v_new, width, dil, max_dist, sink=None):
    cb = cache_ref.shape[2]
    qz = qz.astype(jnp.bfloat16)
    s_c = jnp.dot(qz, cache_ref[b, 0:width, :].astype(jnp.bfloat16),
                  preferred_element_type=jnp.float32) + bias_ref[...]
    s_n = _nt(qz, _pad_rows(k_new).astype(jnp.bfloat16))
    s_n = s_n + _sample_bias(s_n.shape, cb, cb, dil, max_dist)
    m = jnp.maximum(jnp.max(s_c, axis=1, keepdims=True), jnp.max(s_n, axis=1, keepdims=True))
    if sink is not None:
        m = jnp.maximum(m, sink)
    p_c, p_n = jnp.exp(s_c - m), jnp.exp(s_n - m)
    den = jnp.sum(p_c, axis=1, keepdims=True) + jnp.sum(p_n, axis=1, keepdims=True)
    if sink is not None:
        den = den + jnp.exp(sink - m)
    o = (_nt(p_c.astype(jnp.bfloat16), cache_ref[b, width:2 * width, :].astype(jnp.bfloat16))
         + jnp.dot(p_n.astype(jnp.bfloat16), _pad_rows(v_new).astype(jnp.bfloat16),
                   preferred_element_type=jnp.float32))
    return o / den, m + jnp.log(den)


def _shift_cache(cache_ref, new_ref, b, new_rows):
    cb = cache_ref.shape[2]
    lane = lax.broadcasted_iota(jnp.int32, (cache_ref.shape[1], LANES), 1)
    keep = lane < LANES - DEC_SEQ
    tail = jnp.transpose(_pad_rows(new_rows, LANES, LANES - DEC_SEQ))
    cur = pltpu.roll(cache_ref[b, :, 0:LANES], LANES - DEC_SEQ, 1)
    for j in range(cb // LANES):
        if j + 1 < cb // LANES:
            nxt = pltpu.roll(cache_ref[b, :, (j + 1) * LANES:(j + 2) * LANES], LANES - DEC_SEQ, 1)
        else:
            nxt = tail
        new_ref[b, :, j * LANES:(j + 1) * LANES] = jnp.where(keep, cur, nxt)
        cur = nxt


def _sample_kernel(*refs):
    cache_refs, bias_refs = refs[1:5], refs[-4:]

    @pl.when(pl.program_id(0) == 0)
    def _():
        windows = [(1, A_WINDOW - 1)] + [(dil, win) for win, dil in B_PAIRS]
        for cache, bias, (dil, max_dist) in zip(cache_refs, bias_refs, windows):
            bias[...] = _sample_bias(bias.shape, 0, cache.shape[2], dil, max_dist)

    for b in range(_SAMPLE_SEQS_PER_STEP):
        _sample_one(b, *refs)


def _sample_one(b, h_ref, ca_ref, c0_ref, c1_ref, c2_ref, sink_ref,
                oa_ref, ob_ref, na_ref, n0_ref, n1_ref, n2_ref, ba_ref, b0_ref, b1_ref, b2_ref):
    lo = lax.broadcasted_iota(jnp.int32, (DEC_SEQ, LANES), 1) < HALF
    rows = slice(b * DEC_SEQ, (b + 1) * DEC_SEQ)
    slabs = lambda name, count: [h_ref[_SEG[name][0], _SEG[name][1] + c, rows, :]
                                 for c in range(count)]

    group = A_Q_HEADS // A_KV_HEADS
    pieces = []
    for pair, slab in enumerate(slabs("qa", A_WIDTH // LANES)):
        for sub in range(2):
            pieces.append(_to_half(slab * SCALE, sub, (2 * pair + sub) // group, lo))
    (k_new,), (v_new,) = slabs("ka", 1), slabs("va", 1)
    o, _ = _sample_attend(jnp.concatenate(pieces, axis=0), ca_ref, b, ba_ref, k_new, v_new,
                          A_KV_WIDTH, 1, A_WINDOW - 1, sink_ref[...])
    for pair in range(A_Q_HEADS // 2):
        oa_ref[pair, rows, :] = _merge_halves(o[(2 * pair) * DEC_SEQ:(2 * pair + 1) * DEC_SEQ, :],
                                     o[(2 * pair + 1) * DEC_SEQ:(2 * pair + 2) * DEC_SEQ, :],
                                     (2 * pair) // group, lo)
    _shift_cache(ca_ref, na_ref, b, jnp.concatenate([k_new, v_new], axis=1))

    n_col = B_KV_WIDTH // LANES
    zeros = jnp.zeros((DEC_SEQ, LANES), jnp.float32)
    outs, lses = [], []
    for gi, ((win, dil), c_ref, n_ref, bias_ref) in enumerate(
            zip(B_PAIRS, (c0_ref, c1_ref, c2_ref), (n0_ref, n1_ref, n2_ref),
                (b0_ref, b1_ref, b2_ref))):
        pieces = []
        for g, slab in enumerate(slabs(f"qb{gi}", B_WIDTH // LANES)):
            for sub in range(2):
                cols = [zeros] * n_col
                cols[g // 2] = _to_half(slab * SCALE, sub, g % 2, lo)
                pieces.append(jnp.concatenate(cols, axis=1))
        k_new = jnp.concatenate(slabs(f"kb{gi}", n_col), axis=1)
        v_new = jnp.concatenate(slabs(f"vb{gi}", n_col), axis=1)
        o, lse = _sample_attend(jnp.concatenate(pieces, axis=0), c_ref, b, bias_ref, k_new, v_new,
                                B_KV_WIDTH, dil, win)
        outs.append(o), lses.append(lse)
        _shift_cache(c_ref, n_ref, b, jnp.concatenate([k_new, v_new], axis=1))
    top = functools.reduce(jnp.maximum, lses)
    es = [jnp.exp(v - top) for v in lses]
    tot = functools.reduce(lambda a, c: a + c, es)
    o = functools.reduce(lambda a, c: a + c, [(e / tot) * v for e, v in zip(es, outs)])
    for g in range(B_KV_HEADS):
        col = slice((g // 2) * LANES, (g // 2 + 1) * LANES)
        ob_ref[g, rows, :] = _merge_halves(o[(2 * g) * DEC_SEQ:(2 * g + 1) * DEC_SEQ, col],
                                  o[(2 * g + 1) * DEC_SEQ:(2 * g + 2) * DEC_SEQ, col], g % 2, lo)


def _sample_step(hs3, caches_t, sink_rows):
    n = caches_t[0].shape[0]
    per_step = _SAMPLE_SEQS_PER_STEP
    cache_specs = [pl.BlockSpec((per_step,) + c.shape[1:], lambda i: (i, 0, 0)) for c in caches_t]
    row_blk = lambda slabs: pl.BlockSpec((slabs, per_step * DEC_SEQ, LANES), lambda i: (0, i, 0))
    return pl.pallas_call(
        _sample_kernel,
        grid=(n // per_step,),
        in_specs=[pl.BlockSpec((len(_PARTS), PART_SLABS, per_step * DEC_SEQ, LANES),
                               lambda i: (0, 0, i, 0))] + cache_specs
        + [pl.BlockSpec((A_Q_HEADS * DEC_SEQ, 1), lambda i: (0, 0))],
        out_specs=[row_blk(A_WIDTH // LANES), row_blk(B_WIDTH // LANES)] + cache_specs,
        out_shape=[jax.ShapeDtypeStruct((A_WIDTH // LANES, n * DEC_SEQ, LANES), jnp.float32),
                   jax.ShapeDtypeStruct((B_WIDTH // LANES, n * DEC_SEQ, LANES), jnp.float32)]
        + [jax.ShapeDtypeStruct(c.shape, jnp.float32) for c in caches_t],
        scratch_shapes=[pltpu.VMEM((rows * DEC_SEQ, c.shape[2]), jnp.float32)
                        for rows, c in zip([A_Q_HEADS] + [B_Q_HEADS] * 3, caches_t)],
        compiler_params=_params(1),
        name="sample_step",
    )(hs3, *caches_t, sink_rows)


def _sigmoid(x):
    return 1.0 / (1.0 + jnp.exp(-x))


def _out_kernel(*refs, n_groups):
    x_ref, oa_ref, za_ref, ga_ref, gb_ref, zb_ref = refs[:6]
    ob_refs = refs[6:6 + n_groups]
    lse_refs = refs[6 + n_groups:6 + 2 * n_groups] if n_groups > 1 else ()
    wa_ref, wb_ref, wo_ref, lng_ref, lnb_ref, y_ref = refs[-6:]

    def gated(o, z):
        return (o * (z * _sigmoid(z))).astype(jnp.bfloat16)

    ya = jnp.concatenate([gated(oa_ref[s], za_ref[s]) for s in range(A_WIDTH // LANES)], axis=1)
    yb = []
    for s in range(B_WIDTH // LANES):
        if n_groups > 1:
            lses = [r[s] for r in lse_refs]
            top = functools.reduce(jnp.maximum, lses)
            es = [jnp.exp(v - top) for v in lses]
            tot = functools.reduce(lambda a, c: a + c, es)
            ob = functools.reduce(lambda a, c: a + c,
                                  [(e / tot) * r[s] for e, r in zip(es, ob_refs)])
        else:
            ob = ob_refs[0][s]
        yb.append(gated(ob, zb_ref[s]))
    yb = jnp.concatenate(yb, axis=1)
    da = jnp.dot(ya, wa_ref[...], preferred_element_type=jnp.float32)
    db = jnp.dot(yb, wb_ref[...], preferred_element_type=jnp.float32)
    m = jnp.concatenate(
        [(_sigmoid(ga_ref[s]) * da[:, s * LANES:(s + 1) * LANES]
          + _sigmoid(gb_ref[s]) * db[:, s * LANES:(s + 1) * LANES]).astype(jnp.bfloat16)
         for s in range(D_MODEL // LANES)], axis=1)
    h = DN_ALPHA * x_ref[...] + jnp.dot(m, wo_ref[...], preferred_element_type=jnp.float32)
    mu = jnp.mean(h, axis=-1, keepdims=True)
    var = jnp.mean(jnp.square(h - mu), axis=-1, keepdims=True)
    y_ref[...] = (h - mu) * lax.rsqrt(var + LN_EPS) * lng_ref[...] + lnb_ref[...]


def _out_proj(x, h3, oa, obs, lses, wa, wb, wo, ln_g, ln_b, tm=256):
    m = x.shape[0]
    n_groups = len(obs)

    mixer_out = lambda count: pl.BlockSpec((count, tm, LANES), lambda i: (0, i, 0))
    seg = lambda name, count: _seg_spec(name, count, tm, lambda i: i)
    row = pl.BlockSpec((tm, D_MODEL), lambda i: (i, 0))
    whole = lambda a: pl.BlockSpec(a.shape, lambda i: (0, 0), pipeline_mode=pl.Buffered(1))
    a_slabs, b_slabs, d_slabs = A_WIDTH // LANES, B_WIDTH // LANES, D_MODEL // LANES
    in_specs = ([row, mixer_out(a_slabs), seg("za", a_slabs), seg("ga", d_slabs),
                 seg("gb", d_slabs), seg("zb", b_slabs)]
                + [mixer_out(b_slabs)] * (len(obs) + len(lses))
                + [whole(wa), whole(wb), whole(wo), whole(ln_g), whole(ln_b)])
    return pl.pallas_call(
        functools.partial(_out_kernel, n_groups=n_groups),
        grid=(m // tm,),
        in_specs=in_specs,
        out_specs=row,
        out_shape=jax.ShapeDtypeStruct((m, D_MODEL), jnp.float32),
        compiler_params=_params(1),
        name=f"out_proj{n_groups}",
    )(x, oa, h3, h3, h3, h3, *obs, *lses, wa, wb, wo, ln_g, ln_b)


_UNIT = _PAIR * LANES
assert all(_COL_ORDER[u * _UNIT] % _UNIT == 0
           and np.all(np.diff(_COL_ORDER[u * _UNIT:(u + 1) * _UNIT]) == 1)
           for u in range(IN_WIDTH // _UNIT))


def _cast_permute_kernel(order_ref, *refs):
    del order_ref
    o_ref = refs[-1]
    for u, w_ref in enumerate(refs[:-1]):
        o_ref[:, u * _UNIT:(u + 1) * _UNIT] = w_ref[...].astype(jnp.bfloat16)


def _cast_permute_w(w):
    k, n = w.shape
    per_step = len(_PARTS)
    unit_order = jnp.asarray(_COL_ORDER[::_UNIT] // _UNIT, jnp.int32)
    unit_spec = lambda u: pl.BlockSpec((k, _UNIT), lambda j, order: (0, order[j * per_step + u]))
    return pl.pallas_call(
        _cast_permute_kernel,
        grid_spec=pltpu.PrefetchScalarGridSpec(
            num_scalar_prefetch=1, grid=(n // (_UNIT * per_step),),
            in_specs=[unit_spec(u) for u in range(per_step)],
            out_specs=pl.BlockSpec((k, _UNIT * per_step), lambda j, order: (0, j))),
        out_shape=jax.ShapeDtypeStruct((k, n), jnp.bfloat16),
        compiler_params=_params(1),
        name="cast_permute_w",
    )(unit_order, *([w] * per_step))


def kernel(x_prompt, x_sample, cache_a_kv, cache_b0_kv, cache_b1_kv, cache_b2_kv,
           w_in, b_in, sink_a, w_br_a, w_br_b, w_out, ln_g, ln_b):
    n, l, _ = x_prompt.shape
    ns, t, _ = x_sample.shape
    assert w_in.shape[0] == 1 and t == DEC_SEQ and l == SEQ
    assert PAST_LEN >= max(w for w, _ in B_PAIRS)
    w_bf = _cast_permute_w(w_in[0])
    b_p = jnp.take(b_in[0], jnp.asarray(_COL_ORDER, jnp.int32))[None, :]
    wa, wb, wo = (w[0].astype(jnp.bfloat16) for w in (w_br_a, w_br_b, w_out))
    tabs_p = _rope_tables(jnp.arange(l, dtype=jnp.float32))
    tabs_s = [jnp.tile(tb, (ns, 1)) for tb in
              _rope_tables(PAST_LEN + jnp.arange(t, dtype=jnp.float32))]
    sink = sink_a[0]
    sink_rows = jnp.repeat(sink, DEC_SEQ)[:, None]

    xp = x_prompt.reshape(n * l, D_MODEL)
    hp = _in_proj(xp, w_bf, b_p, tabs_p, tm=2048)
    oa = _attn_a(hp, sink, n, l)
    st_a = _kv_state(hp, "ka", "va", A_KV_WIDTH, min(A_WINDOW, l), n, l, "state_a")
    obs, lses, st_b = [], [], []
    for gi, (win, dil) in enumerate(B_PAIRS):
        o, lse = _attn_b(hp, gi, dil, n, l, n_sub={1: 4, 4: 2, 16: 1}[dil])
        obs.append(o), lses.append(lse)
        st_b.append(_kv_state(hp, f"kb{gi}", f"vb{gi}", B_KV_WIDTH, min(win, l), n, l,
                              f"state_b{gi}"))
    yp = _out_proj(xp, hp, oa, obs, lses, wa, wb, wo, ln_g, ln_b)

    xs = x_sample.reshape(ns * t, D_MODEL)
    hs = _in_proj(xs, w_bf, b_p, tabs_s, tm=1024)
    to_t = lambda c, w: jnp.transpose(c[0].reshape(ns, -1, w), (0, 2, 1))
    caches_t = [to_t(cache_a_kv, 2 * A_KV_WIDTH)] + [
        to_t(c, 2 * B_KV_WIDTH) for c in (cache_b0_kv, cache_b1_kv, cache_b2_kv)]
    oa_s, ob_s, *new_t = _sample_step(hs, caches_t, sink_rows)
    ys = _out_proj(xs, hs, oa_s, [ob_s], [], wa, wb, wo, ln_g, ln_b)

    kv_shape = lambda a, heads: a.reshape((1, a.shape[0], a.shape[1], 2, heads, HEAD_DIM))
    from_t = lambda a, heads: kv_shape(jnp.transpose(a, (0, 2, 1)), heads)
    return (yp.reshape(n, l, D_MODEL), ys.reshape(ns, t, D_MODEL),
            from_t(st_a, A_KV_HEADS), *[from_t(s, B_KV_HEADS) for s in st_b],
            from_t(new_t[0], A_KV_HEADS), *[from_t(c, B_KV_HEADS) for c in new_t[1:]])
```

```python
import functools

import jax
import jax.numpy as jnp
import numpy as np
from jax import lax
from jax.experimental import pallas as pl
from jax.experimental.pallas import tpu as pltpu

D_MODEL = 2048
SEQ = 8192
DEC_SEQ = 8
PAST_LEN = 16384
HEAD_DIM = 64
ROT_DIM = HEAD_DIM // 4
ROPE_THETA = 500000.0
BLK = 128
A_WINDOW = 128
A_Q_HEADS = 16
A_KV_HEADS = 2
B_PAIRS = ((128, 1), (512, 4), (2048, 16))
B_Q_HEADS = 8
B_KV_HEADS = 4
A_WIDTH = A_Q_HEADS * HEAD_DIM
B_WIDTH = B_Q_HEADS * HEAD_DIM
A_KV_WIDTH = A_KV_HEADS * HEAD_DIM
B_KV_WIDTH = B_KV_HEADS * HEAD_DIM
SCALE = HEAD_DIM ** -0.5
DN_ALPHA = 2.0 ** 0.25
LN_EPS = 1e-5
LANES = 128
HALF = LANES // 2

_ORIG = {}
_off = 0
for _name, _w in ([("qa", A_WIDTH), ("ka", A_KV_WIDTH), ("va", A_KV_WIDTH), ("za", A_WIDTH)]
                  + [(f"{p}b{i}", w) for i in range(3)
                     for p, w in (("q", B_WIDTH), ("k", B_KV_WIDTH), ("v", B_KV_WIDTH))]
                  + [("zb", B_WIDTH), ("ga", D_MODEL), ("gb", D_MODEL)]):
    _ORIG[_name] = (_off, _w)
    _off += _w
IN_WIDTH = _off

_PARTS = (("qa", "qb0", "qb1", "qb2", "kb0", "kb1", "kb2"),
          ("gb", "zb", "ka", "va", "vb0", "vb1"),
          ("ga", "za", "vb2"))
_ROTARY = {"qa", "qb0", "qb1", "qb2", "kb0", "kb1", "kb2", "ka"}
_PAIR = 2
PART_SLABS = IN_WIDTH // LANES // len(_PARTS)
_SEG = {}
_PART_COLS = []
for _part, _names in enumerate(_PARTS):
    _off, _cols = 0, []
    for _name in _names:
        _slabs = _ORIG[_name][1] // LANES
        assert _off % min(_slabs, 16) == 0
        _SEG[_name] = (_part, _off)
        _cols.append(np.arange(_ORIG[_name][0], _ORIG[_name][0] + _ORIG[_name][1]))
        _off += _slabs
    assert _off == PART_SLABS
    _PART_COLS.append(np.concatenate(_cols))
assert _SEG["va"] == (_SEG["ka"][0], _SEG["ka"][1] + 1) and _SEG["ka"][1] % 2 == 0
N_TILES = PART_SLABS // _PAIR
_TILE_COLS = _PAIR * LANES * len(_PARTS)
_COL_ORDER = np.concatenate([_PART_COLS[p][j * _PAIR * LANES:(j + 1) * _PAIR * LANES]
                             for j in range(N_TILES) for p in range(len(_PARTS))])


def _tile_rotary_flags(j):
    flags = []
    for part, names in enumerate(_PARTS):
        for t in range(_PAIR):
            slab = j * _PAIR + t
            name = next(n for n in names
                        if _SEG[n][1] <= slab < _SEG[n][1] + _ORIG[n][1] // LANES)
            flags.append(name in _ROTARY)
    return tuple(flags)


def _seg_spec(name, count, rows, row_block, **kwargs):
    part, first = _SEG[name]
    assert first % count == 0
    return pl.BlockSpec((None, count, rows, LANES),
                        lambda *g: (part, first // count, row_block(*g), 0), **kwargs)

_VMEM_LIMIT = 56 * 1024 * 1024


def _params(n_axes, vmem=_VMEM_LIMIT):
    return pltpu.CompilerParams(dimension_semantics=("arbitrary",) * n_axes,
                                vmem_limit_bytes=vmem)


def _rope_tables(pos):
    half = ROT_DIM // 2
    inv = ROPE_THETA ** (-jnp.arange(0, ROT_DIM, 2, dtype=jnp.float32) / ROT_DIM)
    ang = pos[:, None] * inv[None, :]
    cos, sin = jnp.cos(ang), jnp.sin(ang)
    dim = np.arange(LANES) % HEAD_DIM
    first, second = dim < half, (dim >= half) & (dim < ROT_DIM)

    def spread(values, lanes, sign=1.0):
        pick = (np.arange(half)[:, None] == (dim % half)[None, :]) & lanes[None, :]
        return jnp.dot(values, jnp.asarray(sign * pick, jnp.float32),
                       precision=lax.Precision.HIGHEST)

    c = spread(cos, first | second) + jnp.asarray(~(first | second), jnp.float32)[None, :]
    s1 = spread(sin, first, -1.0)
    s2 = spread(sin, second)
    return c, s1, s2


def _rope(h, c, s1, s2):
    half = ROT_DIM // 2
    return h * c + pltpu.roll(h, LANES - half, 1) * s1 + pltpu.roll(h, half, 1) * s2


def _to_half(piece, src_half, dst_half, lo):
    if src_half != dst_half:
        piece = pltpu.roll(piece, HALF, 1)
    keep = lo if dst_half == 0 else jnp.logical_not(lo)
    return jnp.where(keep, piece, 0.0)


def _merge_halves(even, odd, src_half, lo):
    if src_half == 1:
        even = pltpu.roll(even, HALF, 1)
    else:
        odd = pltpu.roll(odd, HALF, 1)
    return jnp.where(lo, even, odd)


def _nt(a, b):
    return lax.dot_general(a, b, (((1,), (1,)), ((), ())), preferred_element_type=jnp.float32)


def _band_bias(q_rows, lo_off, hi_off, no_previous):
    i = lax.broadcasted_iota(jnp.int32, (q_rows, BLK + q_rows), 0)
    j = lax.broadcasted_iota(jnp.int32, (q_rows, BLK + q_rows), 1)
    band = (j >= i + lo_off) & (j <= i + hi_off)
    first_key = jnp.where(no_previous, BLK, 0)
    return (jnp.where(band, 0.0, -jnp.inf), jnp.where(band & (j >= first_key), 0.0, -jnp.inf))


def _softmax_pv(s, bias, v_win, sink=None):
    s = s + bias
    m = jnp.max(s, axis=1, keepdims=True)
    if sink is not None:
        m = jnp.maximum(m, sink)
    p = jnp.exp(s - m)
    den = jnp.sum(p, axis=1, keepdims=True)
    if sink is not None:
        den = den + jnp.exp(sink - m)
    o = jnp.dot(p.astype(jnp.bfloat16), v_win, preferred_element_type=jnp.float32)
    return o * (1.0 / den), m + jnp.log(den)


def _in_proj_kernel(x_ref, w_ref, b_ref, c_ref, s1_ref, s2_ref, o_ref, xb_ref):
    j = pl.program_id(1)

    @pl.when(j == 0)
    def _():
        xb_ref[...] = x_ref[...].astype(jnp.bfloat16)

    def tile(rotary):
        for part in range(len(_PARTS)):
            cols = slice(part * _PAIR * LANES, (part + 1) * _PAIR * LANES)
            acc = jnp.dot(xb_ref[...], w_ref[:, cols],
                          preferred_element_type=jnp.float32) + b_ref[:, cols]
            for t in range(_PAIR):
                col = acc[:, t * LANES:(t + 1) * LANES]
                if rotary[part * _PAIR + t]:
                    col = _rope(col, c_ref[...], s1_ref[...], s2_ref[...])
                o_ref[part, t] = col

    patterns = {}
    for tile_index in range(N_TILES):
        patterns.setdefault(_tile_rotary_flags(tile_index), []).append(tile_index)
    for rotary, tiles in patterns.items():
        hit = functools.reduce(jnp.logical_or, [j == t for t in tiles])
        pl.when(hit)(functools.partial(tile, rotary))


def _in_proj(x, w_bf, b, tabs, tm=1024):
    m, k = x.shape
    n_pos_blocks = tabs[0].shape[0] // tm
    tab = pl.BlockSpec((tm, LANES), lambda i, j: (i % n_pos_blocks, 0))
    return pl.pallas_call(
        _in_proj_kernel,
        grid=(m // tm, N_TILES),
        in_specs=[pl.BlockSpec((tm, k), lambda i, j: (i, 0)),
                  pl.BlockSpec((k, _TILE_COLS), lambda i, j: (0, j)),
                  pl.BlockSpec((1, _TILE_COLS), lambda i, j: (0, j)), tab, tab, tab],
        out_specs=pl.BlockSpec((len(_PARTS), _PAIR, tm, LANES), lambda i, j: (0, j, i, 0)),
        out_shape=jax.ShapeDtypeStruct((len(_PARTS), PART_SLABS, m, LANES), jnp.float32),
        scratch_shapes=[pltpu.VMEM((tm, k), jnp.bfloat16)],
        compiler_params=_params(2),
        name="in_proj",
    )(x, w_bf, b, *tabs)


_UNITS_PER_TRIP = 2
_UNIT_ROWS = BLK


def _attend_pairs(slabs, k_wins, v_wins, kv_of_pair, bias, lo, sinks=None):
    result = []
    for p, slab in enumerate(slabs):
        col, half = kv_of_pair[p]
        slab = slab * SCALE
        outs, lses = [], []
        for sub in range(2):
            q_h = _to_half(slab, sub, half, lo).astype(jnp.bfloat16)
            o, lse = _softmax_pv(_nt(q_h, k_wins[col]), bias, v_wins[col],
                                 None if sinks is None else sinks[2 * p + sub])
            outs.append(o), lses.append(lse)
        result.append((_merge_halves(outs[0], outs[1], half, lo),
                       jnp.where(lo, jnp.broadcast_to(lses[0], lo.shape),
                                 jnp.broadcast_to(lses[1], lo.shape))))
    return result


def _attn_a_kernel(sink_ref, q_ref, kv_ref, kvp_ref, o_ref, k_buf, v_buf, *, n_sub):
    lo = lax.broadcasted_iota(jnp.int32, (_UNIT_ROWS, LANES), 1) < HALF
    seq_start = pl.program_id(1) == 0
    group = A_Q_HEADS // A_KV_HEADS
    for buf, c in ((k_buf, 0), (v_buf, 1)):
        buf[0:BLK, :] = kvp_ref[c].astype(jnp.bfloat16)
        buf[BLK:, :] = kv_ref[c].astype(jnp.bfloat16)
    sinks = [sink_ref[h] for h in range(A_Q_HEADS)]
    kv_of_pair = [(0, (2 * p) // group) for p in range(A_Q_HEADS // 2)]

    def block(j, _):
        start = pl.multiple_of(j * _UNIT_ROWS, _UNIT_ROWS)
        rows = pl.ds(start, _UNIT_ROWS)
        window = pl.ds(start, BLK + _UNIT_ROWS)
        _, bias = _band_bias(_UNIT_ROWS, 1, A_WINDOW, seq_start & (j == 0))
        outs = _attend_pairs([q_ref[p, rows, :] for p in range(A_Q_HEADS // 2)],
                             [k_buf[window, :]], [v_buf[window, :]], kv_of_pair, bias, lo, sinks)
        for p, (o, _) in enumerate(outs):
            o_ref[p, rows, :] = o

    lax.fori_loop(0, n_sub, block, None)


def _attn_a(h3, sink, n, l, n_sub=4):
    rows = _UNIT_ROWS * n_sub
    nsb = l // rows
    q_slabs = A_WIDTH // LANES
    tok = lambda i, sb: i * nsb + sb
    prev_blk = lambda i, sb: jnp.maximum(tok(i, sb) * (rows // BLK) - 1, 0)
    return pl.pallas_call(
        functools.partial(_attn_a_kernel, n_sub=n_sub),
        grid=(n, nsb),
        in_specs=[pl.BlockSpec(memory_space=pltpu.SMEM),
                  _seg_spec("qa", q_slabs, rows, tok),
                  _seg_spec("ka", 2, rows, tok),
                  _seg_spec("ka", 2, BLK, prev_blk)],
        out_specs=pl.BlockSpec((q_slabs, rows, LANES), lambda i, sb: (0, tok(i, sb), 0)),
        out_shape=jax.ShapeDtypeStruct((q_slabs, n * l, LANES), jnp.float32),
        scratch_shapes=[pltpu.VMEM((rows + BLK, LANES), jnp.bfloat16)] * 2,
        compiler_params=_params(2),
        name="attn_a",
    )(sink, h3, h3, h3)


def _attn_b_kernel(q_ref, k_ref, v_ref, kp_ref, vp_ref, o_ref, lse_ref, k_buf, v_buf, *,
                   dil, n_sub):
    n_col = B_KV_WIDTH // LANES
    lo = lax.broadcasted_iota(jnp.int32, (BLK, LANES), 1) < HALF
    seq_start = pl.program_id(1) == 0
    kv_of_pair = [(g // 2, g % 2) for g in range(B_KV_HEADS)]

    def fold(r, j):
        if dil == 1:
            return pl.ds(pl.multiple_of(j * BLK, BLK), BLK)
        return pl.ds(j * (BLK * dil) + r, BLK, stride=dil)

    def fill(r, _):
        for buf, prev, cur in ((k_buf, kp_ref, k_ref), (v_buf, vp_ref, v_ref)):
            for c in range(n_col):
                buf[r, c, 0:BLK, :] = prev[c, fold(r, 0), :].astype(jnp.bfloat16)
                for j in range(n_sub):
                    buf[r, c, (j + 1) * BLK:(j + 2) * BLK, :] = (
                        cur[c, fold(r, j), :].astype(jnp.bfloat16))

    def block(u, _):
        r, j = u // n_sub, u % n_sub
        window = pl.ds(pl.multiple_of(j * BLK, BLK), 2 * BLK)
        _, bias = _band_bias(BLK, 0, BLK, seq_start & (j == 0))
        outs = _attend_pairs([q_ref[g, fold(r, j), :] for g in range(B_KV_HEADS)],
                             [k_buf[r, c, window, :] for c in range(n_col)],
                             [v_buf[r, c, window, :] for c in range(n_col)], kv_of_pair, bias, lo)
        for g, (o, lse) in enumerate(outs):
            o_ref[g, fold(r, j), :] = o
            lse_ref[g, fold(r, j), :] = lse

    if dil == 1:
        fill(0, None)
    else:
        lax.fori_loop(0, dil, fill, None)
    lax.fori_loop(0, dil * n_sub, block, None, unroll=_UNITS_PER_TRIP)


def _attn_b(h3, gi, dil, n, l, n_sub):
    prev_rows = BLK * dil
    rows = prev_rows * n_sub
    nsb = l // rows
    q_slabs = B_WIDTH // LANES
    kv_slabs = B_KV_WIDTH // LANES
    tok = lambda i, sb: i * nsb + sb
    prev_blk = lambda i, sb: jnp.maximum(tok(i, sb) * n_sub - 1, 0)
    out_blk = pl.BlockSpec((q_slabs, rows, LANES), lambda i, sb: (0, tok(i, sb), 0))

    def kv_specs(name):
        return (_seg_spec(f"{name}{gi}", kv_slabs, rows, tok),
                _seg_spec(f"{name}{gi}", kv_slabs, prev_rows, prev_blk))

    (k_cur, k_prev), (v_cur, v_prev) = kv_specs("kb"), kv_specs("vb")
    return pl.pallas_call(
        functools.partial(_attn_b_kernel, dil=dil, n_sub=n_sub),
        grid=(n, nsb),
        in_specs=[_seg_spec(f"qb{gi}", q_slabs, rows, tok), k_cur, v_cur, k_prev, v_prev],
        out_specs=[out_blk, out_blk],
        out_shape=[jax.ShapeDtypeStruct((q_slabs, n * l, LANES), jnp.float32)] * 2,
        scratch_shapes=[pltpu.VMEM((dil, kv_slabs, (n_sub + 1) * BLK, LANES), jnp.bfloat16)] * 2,
        compiler_params=_params(2),
        name=f"attn_b{gi}",
    )(h3, h3, h3, h3, h3)


def _kv_state_kernel(k_ref, v_ref, o_ref):
    n_k = k_ref.shape[0]
    for c in range(n_k):
        o_ref[0, c * LANES:(c + 1) * LANES, :] = jnp.transpose(k_ref[c])
        o_ref[0, (n_k + c) * LANES:(n_k + c + 1) * LANES, :] = jnp.transpose(v_ref[c])


def _kv_state(h3, k_name, v_name, width, win, n, l, name):
    slabs = width // LANES
    chunk = min(win, 4 * BLK)
    per_seq = win // chunk
    first_chunk = lambda i, t: ((i + 1) * l - win) // chunk + t
    spec = lambda s: _seg_spec(s, slabs, chunk, first_chunk)
    return pl.pallas_call(
        _kv_state_kernel,
        grid=(n, per_seq),
        in_specs=[spec(k_name), spec(v_name)],
        out_specs=pl.BlockSpec((1, 2 * width, chunk), lambda i, t: (i, 0, t)),
        out_shape=jax.ShapeDtypeStruct((n, 2 * width, win), jnp.float32),
        compiler_params=_params(2),
        name=name,
    )(h3, h3)


_PAD = 16
_SAMPLE_SEQS_PER_STEP = 2


def _pad_rows(x, rows=_PAD, at=0):
    parts = []
    if at:
        parts.append(jnp.zeros((at, x.shape[1]), x.dtype))
    parts.append(x)
    if rows - at - x.shape[0]:
        parts.append(jnp.zeros((rows - at - x.shape[0], x.shape[1]), x.dtype))
    return jnp.concatenate(parts, axis=0)


def _sample_bias(shape, first_key, cb, dil, max_dist):
    t = lax.broadcasted_iota(jnp.int32, shape, 0) & (DEC_SEQ - 1)
    dist = cb + t - (lax.broadcasted_iota(jnp.int32, shape, 1) + first_key)
    valid = (dist >= 0) & (dist <= max_dist) & ((dist & (dil - 1)) == 0)
    return jnp.where(valid, 0.0, -jnp.inf)


def _sample_attend(qz, cache_ref, b, bias_ref, k_new, v_new, width, dil, max_dist, sink=None):
    cb = cache_ref.shape[2]
    qz = qz.astype(jnp.bfloat16)
    s_c = jnp.dot(qz, cache_ref[b, 0:width, :].astype(jnp.bfloat16),
                  preferred_element_type=jnp.float32) + bias_ref[...]
    s_n = _nt(qz, _pad_rows(k_new).astype(jnp.bfloat16))
    s_n = s_n + _sample_bias(s_n.shape, cb, cb, dil, max_dist)
    m = jnp.maximum(jnp.max(s_c, axis=1, keepdims=True), jnp.max(s_n, axis=1, keepdims=True))
    if sink is not None:
        m = jnp.maximum(m, sink)
    p_c, p_n = jnp.exp(s_c - m), jnp.exp(s_n - m)
    den = jnp.sum(p_c, axis=1, keepdims=True) + jnp.sum(p_n, axis=1, keepdims=True)
    if sink is not None:
        den = den + jnp.exp(sink - m)
    o = (_nt(p_c.astype(jnp.bfloat16), cache_ref[b, width:2 * width, :].astype(jnp.bfloat16))
         + jnp.dot(p_n.astype(jnp.bfloat16), _pad_rows(v_new).astype(jnp.bfloat16),
                   preferred_element_type=jnp.float32))
    return o / den, m + jnp.log(den)


def _shift_cache(cache_ref, new_ref, b, new_rows):
    cb = cache_ref.shape[2]
    lane = lax.broadcasted_iota(jnp.int32, (cache_ref.shape[1], LANES), 1)
    keep = lane < LANES - DEC_SEQ
    tail = jnp.transpose(_pad_rows(new_rows, LANES, LANES - DEC_SEQ))
    cur = pltpu.roll(cache_ref[b, :, 0:LANES], LANES - DEC_SEQ, 1)
    for j in range(cb // LANES):
        if j + 1 < cb // LANES:
            nxt = pltpu.roll(cache_ref[b, :, (j + 1) * LANES:(j + 2) * LANES], LANES - DEC_SEQ, 1)
        else:
            nxt = tail
        new_ref[b, :, j * LANES:(j + 1) * LANES] = jnp.where(keep, cur, nxt)
        cur = nxt


def _sample_kernel(*refs):
    cache_refs, bias_refs = refs[1:5], refs[-4:]

    @pl.when(pl.program_id(0) == 0)
    def _():
        windows = [(1, A_WINDOW - 1)] + [(dil, win) for win, dil in B_PAIRS]
        for cache, bias, (dil, max_dist) in zip(cache_refs, bias_refs, windows):
            bias[...] = _sample_bias(bias.shape, 0, cache.shape[2], dil, max_dist)

    for b in range(_SAMPLE_SEQS_PER_STEP):
        _sample_one(b, *refs)


def _sample_one(b, h_ref, ca_ref, c0_ref, c1_ref, c2_ref, sink_ref,
                oa_ref, ob_ref, na_ref, n0_ref, n1_ref, n2_ref, ba_ref, b0_ref, b1_ref, b2_ref):
    lo = lax.broadcasted_iota(jnp.int32, (DEC_SEQ, LANES), 1) < HALF
    rows = slice(b * DEC_SEQ, (b + 1) * DEC_SEQ)
    slabs = lambda name, count: [h_ref[_SEG[name][0], _SEG[name][1] + c, rows, :]
                                 for c in range(count)]

    group = A_Q_HEADS // A_KV_HEADS
    pieces = []
    for pair, slab in enumerate(slabs("qa", A_WIDTH // LANES)):
        for sub in range(2):
            pieces.append(_to_half(slab * SCALE, sub, (2 * pair + sub) // group, lo))
    (k_new,), (v_new,) = slabs("ka", 1), slabs("va", 1)
    o, _ = _sample_attend(jnp.concatenate(pieces, axis=0), ca_ref, b, ba_ref, k_new, v_new,
                          A_KV_WIDTH, 1, A_WINDOW - 1, sink_ref[...])
    for pair in range(A_Q_HEADS // 2):
        oa_ref[pair, rows, :] = _merge_halves(o[(2 * pair) * DEC_SEQ:(2 * pair + 1) * DEC_SEQ, :],
                                     o[(2 * pair + 1) * DEC_SEQ:(2 * pair + 2) * DEC_SEQ, :],
                                     (2 * pair) // group, lo)
    _shift_cache(ca_ref, na_ref, b, jnp.concatenate([k_new, v_new], axis=1))

    n_col = B_KV_WIDTH // LANES
    zeros = jnp.zeros((DEC_SEQ, LANES), jnp.float32)
    outs, lses = [], []
    for gi, ((win, dil), c_ref, n_ref, bias_ref) in enumerate(
            zip(B_PAIRS, (c0_ref, c1_ref, c2_ref), (n0_ref, n1_ref, n2_ref),
                (b0_ref, b1_ref, b2_ref))):
        pieces = []
        for g, slab in enumerate(slabs(f"qb{gi}", B_WIDTH // LANES)):
            for sub in range(2):
                cols = [zeros] * n_col
                cols[g // 2] = _to_half(slab * SCALE, sub, g % 2, lo)
                pieces.append(jnp.concatenate(cols, axis=1))
        k_new = jnp.concatenate(slabs(f"kb{gi}", n_col), axis=1)
        v_new = jnp.concatenate(slabs(f"vb{gi}", n_col), axis=1)
        o, lse = _sample_attend(jnp.concatenate(pieces, axis=0), c_ref, b, bias_ref, k_new, v_new,
                                B_KV_WIDTH, dil, win)
        outs.append(o), lses.append(lse)
        _shift_cache(c_ref, n_ref, b, jnp.concatenate([k_new, v_new], axis=1))
    top = functools.reduce(jnp.maximum, lses)
    es = [jnp.exp(v - top) for v in lses]
    tot = functools.reduce(lambda a, c: a + c, es)
    o = functools.reduce(lambda a, c: a + c, [(e / tot) * v for e, v in zip(es, outs)])
    for g in range(B_KV_HEADS):
        col = slice((g // 2) * LANES, (g // 2 + 1) * LANES)
        ob_ref[g, rows, :] = _merge_halves(o[(2 * g) * DEC_SEQ:(2 * g + 1) * DEC_SEQ, col],
                                  o[(2 * g + 1) * DEC_SEQ:(2 * g + 2) * DEC_SEQ, col], g % 2, lo)


def _sample_step(hs3, caches_t, sink_rows):
    n = caches_t[0].shape[0]
    per_step = _SAMPLE_SEQS_PER_STEP
    cache_specs = [pl.BlockSpec((per_step,) + c.shape[1:], lambda i: (i, 0, 0)) for c in caches_t]
    row_blk = lambda slabs: pl.BlockSpec((slabs, per_step * DEC_SEQ, LANES), lambda i: (0, i, 0))
    return pl.pallas_call(
        _sample_kernel,
        grid=(n // per_step,),
        in_specs=[pl.BlockSpec((len(_PARTS), PART_SLABS, per_step * DEC_SEQ, LANES),
                               lambda i: (0, 0, i, 0))] + cache_specs
        + [pl.BlockSpec((A_Q_HEADS * DEC_SEQ, 1), lambda i: (0, 0))],
        out_specs=[row_blk(A_WIDTH // LANES), row_blk(B_WIDTH // LANES)] + cache_specs,
        out_shape=[jax.ShapeDtypeStruct((A_WIDTH // LANES, n * DEC_SEQ, LANES), jnp.float32),
                   jax.ShapeDtypeStruct((B_WIDTH // LANES, n * DEC_SEQ, LANES), jnp.float32)]
        + [jax.ShapeDtypeStruct(c.shape, jnp.float32) for c in caches_t],
        scratch_shapes=[pltpu.VMEM((rows * DEC_SEQ, c.shape[2]), jnp.float32)
                        for rows, c in zip([A_Q_HEADS] + [B_Q_HEADS] * 3, caches_t)],
        compiler_params=_params(1),
        name="sample_step",
    )(hs3, *caches_t, sink_rows)


def _sigmoid(x):
    return 1.0 / (1.0 + jnp.exp(-x))


def _out_kernel(*refs, n_groups):
    x_ref, oa_ref, za_ref, ga_ref, gb_ref, zb_ref = refs[:6]
    ob_refs = refs[6:6 + n_groups]
    lse_refs = refs[6 + n_groups:6 + 2 * n_groups] if n_groups > 1 else ()
    wa_ref, wb_ref, wo_ref, lng_ref, lnb_ref, y_ref = refs[-6:]

    def gated(o, z):
        return (o * (z * _sigmoid(z))).astype(jnp.bfloat16)

    ya = jnp.concatenate([gated(oa_ref[s], za_ref[s]) for s in range(A_WIDTH // LANES)], axis=1)
    yb = []
    for s in range(B_WIDTH // LANES):
        if n_groups > 1:
            lses = [r[s] for r in lse_refs]
            top = functools.reduce(jnp.maximum, lses)
            es = [jnp.exp(v - top) for v in lses]
            tot = functools.reduce(lambda a, c: a + c, es)
            ob = functools.reduce(lambda a, c: a + c,
                                  [(e / tot) * r[s] for e, r in zip(es, ob_refs)])
        else:
            ob = ob_refs[0][s]
        yb.append(gated(ob, zb_ref[s]))
    yb = jnp.concatenate(yb, axis=1)
    da = jnp.dot(ya, wa_ref[...], preferred_element_type=jnp.float32)
    db = jnp.dot(yb, wb_ref[...], preferred_element_type=jnp.float32)
    m = jnp.concatenate(
        [(_sigmoid(ga_ref[s]) * da[:, s * LANES:(s + 1) * LANES]
          + _sigmoid(gb_ref[s]) * db[:, s * LANES:(s + 1) * LANES]).astype(jnp.bfloat16)
         for s in range(D_MODEL // LANES)], axis=1)
    h = DN_ALPHA * x_ref[...] + jnp.dot(m, wo_ref[...], preferred_element_type=jnp.float32)
    mu = jnp.mean(h, axis=-1, keepdims=True)
    var = jnp.mean(jnp.square(h - mu), axis=-1, keepdims=True)
    y_ref[...] = (h - mu) * lax.rsqrt(var + LN_EPS) * lng_ref[...] + lnb_ref[...]


def _out_proj(x, h3, oa, obs, lses, wa, wb, wo, ln_g, ln_b, tm=256):
    m = x.shape[0]
    n_groups = len(obs)

    mixer_out = lambda count: pl.BlockSpec((count, tm, LANES), lambda i: (0, i, 0))
    seg = lambda name, count: _seg_spec(name, count, tm, lambda i: i)
    row = pl.BlockSpec((tm, D_MODEL), lambda i: (i, 0))
    whole = lambda a: pl.BlockSpec(a.shape, lambda i: (0, 0), pipeline_mode=pl.Buffered(1))
    a_slabs, b_slabs, d_slabs = A_WIDTH // LANES, B_WIDTH // LANES, D_MODEL // LANES
    in_specs = ([row, mixer_out(a_slabs), seg("za", a_slabs), seg("ga", d_slabs),
                 seg("gb", d_slabs), seg("zb", b_slabs)]
                + [mixer_out(b_slabs)] * (len(obs) + len(lses))
                + [whole(wa), whole(wb), whole(wo), whole(ln_g), whole(ln_b)])
    return pl.pallas_call(
        functools.partial(_out_kernel, n_groups=n_groups),
        grid=(m // tm,),
        in_specs=in_specs,
        out_specs=row,
        out_shape=jax.ShapeDtypeStruct((m, D_MODEL), jnp.float32),
        compiler_params=_params(1),
        name=f"out_proj{n_groups}",
    )(x, oa, h3, h3, h3, h3, *obs, *lses, wa, wb, wo, ln_g, ln_b)


_UNIT = _PAIR * LANES
assert all(_COL_ORDER[u * _UNIT] % _UNIT == 0
           and np.all(np.diff(_COL_ORDER[u * _UNIT:(u + 1) * _UNIT]) == 1)
           for u in range(IN_WIDTH // _UNIT))


def _cast_permute_kernel(order_ref, *refs):
    del order_ref
    o_ref = refs[-1]
    for u, w_ref in enumerate(refs[:-1]):
        o_ref[:, u * _UNIT:(u + 1) * _UNIT] = w_ref[...].astype(jnp.bfloat16)


def _cast_permute_w(w):
    k, n = w.shape
    per_step = len(_PARTS)
    unit_order = jnp.asarray(_COL_ORDER[::_UNIT] // _UNIT, jnp.int32)
    unit_spec = lambda u: pl.BlockSpec((k, _UNIT), lambda j, order: (0, order[j * per_step + u]))
    return pl.pallas_call(
        _cast_permute_kernel,
        grid_spec=pltpu.PrefetchScalarGridSpec(
            num_scalar_prefetch=1, grid=(n // (_UNIT * per_step),),
            in_specs=[unit_spec(u) for u in range(per_step)],
            out_specs=pl.BlockSpec((k, _UNIT * per_step), lambda j, order: (0, j))),
        out_shape=jax.ShapeDtypeStruct((k, n), jnp.bfloat16),
        compiler_params=_params(1),
        name="cast_permute_w",
    )(unit_order, *([w] * per_step))


def kernel(x_prompt, x_sample, cache_a_kv, cache_b0_kv, cache_b1_kv, cache_b2_kv,
           w_in, b_in, sink_a, w_br_a, w_br_b, w_out, ln_g, ln_b):
    n, l, _ = x_prompt.shape
    ns, t, _ = x_sample.shape
    assert w_in.shape[0] == 1 and t == DEC_SEQ and l == SEQ
    assert PAST_LEN >= max(w for w, _ in B_PAIRS)
    w_bf = _cast_permute_w(w_in[0])
    b_p = jnp.take(b_in[0], jnp.asarray(_COL_ORDER, jnp.int32))[None, :]
    wa, wb, wo = (w[0].astype(jnp.bfloat16) for w in (w_br_a, w_br_b, w_out))
    tabs_p = _rope_tables(jnp.arange(l, dtype=jnp.float32))
    tabs_s = [jnp.tile(tb, (ns, 1)) for tb in
              _rope_tables(PAST_LEN + jnp.arange(t, dtype=jnp.float32))]
    sink = sink_a[0]
    sink_rows = jnp.repeat(sink, DEC_SEQ)[:, None]

    xp = x_prompt.reshape(n * l, D_MODEL)
    hp = _in_proj(xp, w_bf, b_p, tabs_p)
    oa = _attn_a(hp, sink, n, l)
    st_a = _kv_state(hp, "ka", "va", A_KV_WIDTH, min(A_WINDOW, l), n, l, "state_a")
    obs, lses, st_b = [], [], []
    for gi, (win, dil) in enumerate(B_PAIRS):
        o, lse = _attn_b(hp, gi, dil, n, l, n_sub={1: 4, 4: 2, 16: 1}[dil])
        obs.append(o), lses.append(lse)
        st_b.append(_kv_state(hp, f"kb{gi}", f"vb{gi}", B_KV_WIDTH, min(win, l), n, l,
                              f"state_b{gi}"))
    yp = _out_proj(xp, hp, oa, obs, lses, wa, wb, wo, ln_g, ln_b)

    xs = x_sample.reshape(ns * t, D_MODEL)
    hs = _in_proj(xs, w_bf, b_p, tabs_s)
    to_t = lambda c, w: jnp.transpose(c[0].reshape(ns, -1, w), (0, 2, 1))
    caches_t = [to_t(cache_a_kv, 2 * A_KV_WIDTH)] + [
        to_t(c, 2 * B_KV_WIDTH) for c in (cache_b0_kv, cache_b1_kv, cache_b2_kv)]
    oa_s, ob_s, *new_t = _sample_step(hs, caches_t, sink_rows)
    ys = _out_proj(xs, hs, oa_s, [ob_s], [], wa, wb, wo, ln_g, ln_b)

    kv_shape = lambda a, heads: a.reshape((1, a.shape[0], a.shape[1], 2, heads, HEAD_DIM))
    from_t = lambda a, heads: kv_shape(jnp.transpose(a, (0, 2, 1)), heads)
    return (yp.reshape(n, l, D_MODEL), ys.reshape(ns, t, D_MODEL),
            from_t(st_a, A_KV_HEADS), *[from_t(s, B_KV_HEADS) for s in st_b],
            from_t(new_t[0], A_KV_HEADS), *[from_t(c, B_KV_HEADS) for c in new_t[1:]])
```

```python
import functools

import jax
import jax.numpy as jnp
import numpy as np
from jax import lax
from jax.experimental import pallas as pl
from jax.experimental.pallas import tpu as pltpu

D_MODEL = 2048
SEQ = 8192
DEC_SEQ = 8
PAST_LEN = 16384
HEAD_DIM = 64
ROT_DIM = HEAD_DIM // 4
ROPE_THETA = 500000.0
BLK = 128
A_WINDOW = 128
A_Q_HEADS = 16
A_KV_HEADS = 2
B_PAIRS = ((128, 1), (512, 4), (2048, 16))
B_Q_HEADS = 8
B_KV_HEADS = 4
A_WIDTH = A_Q_HEADS * HEAD_DIM
B_WIDTH = B_Q_HEADS * HEAD_DIM
A_KV_WIDTH = A_KV_HEADS * HEAD_DIM
B_KV_WIDTH = B_KV_HEADS * HEAD_DIM
SCALE = HEAD_DIM ** -0.5
DN_ALPHA = 2.0 ** 0.25
LN_EPS = 1e-5
LANES = 128
HALF = LANES // 2

_ORIG = {}
_off = 0
for _name, _w in ([("qa", A_WIDTH), ("ka", A_KV_WIDTH), ("va", A_KV_WIDTH), ("za", A_WIDTH)]
                  + [(f"{p}b{i}", w) for i in range(3)
                     for p, w in (("q", B_WIDTH), ("k", B_KV_WIDTH), ("v", B_KV_WIDTH))]
                  + [("zb", B_WIDTH), ("ga", D_MODEL), ("gb", D_MODEL)]):
    _ORIG[_name] = (_off, _w)
    _off += _w
IN_WIDTH = _off

_PARTS = (("qa", "qb0", "qb1", "qb2", "kb0", "kb1", "kb2"),
          ("gb", "zb", "ka", "va", "vb0", "vb1"),
          ("ga", "za", "vb2"))
_ROTARY = {"qa", "qb0", "qb1", "qb2", "kb0", "kb1", "kb2", "ka"}
_PAIR = 2
PART_SLABS = IN_WIDTH // LANES // len(_PARTS)
_SEG = {}
_PART_COLS = []
for _part, _names in enumerate(_PARTS):
    _off, _cols = 0, []
    for _name in _names:
        _slabs = _ORIG[_name][1] // LANES
        assert _off % min(_slabs, 16) == 0
        _SEG[_name] = (_part, _off)
        _cols.append(np.arange(_ORIG[_name][0], _ORIG[_name][0] + _ORIG[_name][1]))
        _off += _slabs
    assert _off == PART_SLABS
    _PART_COLS.append(np.concatenate(_cols))
assert _SEG["va"] == (_SEG["ka"][0], _SEG["ka"][1] + 1) and _SEG["ka"][1] % 2 == 0
N_TILES = PART_SLABS // _PAIR
_TILE_COLS = _PAIR * LANES * len(_PARTS)
_COL_ORDER = np.concatenate([_PART_COLS[p][j * _PAIR * LANES:(j + 1) * _PAIR * LANES]
                             for j in range(N_TILES) for p in range(len(_PARTS))])


def _tile_rotary_flags(j):
    flags = []
    for part, names in enumerate(_PARTS):
        for t in range(_PAIR):
            slab = j * _PAIR + t
            name = next(n for n in names
                        if _SEG[n][1] <= slab < _SEG[n][1] + _ORIG[n][1] // LANES)
            flags.append(name in _ROTARY)
    return tuple(flags)


def _seg_spec(name, count, rows, row_block, **kwargs):
    part, first = _SEG[name]
    assert first % count == 0
    return pl.BlockSpec((None, count, rows, LANES),
                        lambda *g: (part, first // count, row_block(*g), 0), **kwargs)

_VMEM_LIMIT = 56 * 1024 * 1024


def _params(n_axes, vmem=_VMEM_LIMIT):
    return pltpu.CompilerParams(dimension_semantics=("arbitrary",) * n_axes,
                                vmem_limit_bytes=vmem)


def _rope_tables(pos):
    half = ROT_DIM // 2
    inv = ROPE_THETA ** (-jnp.arange(0, ROT_DIM, 2, dtype=jnp.float32) / ROT_DIM)
    dim = np.arange(LANES) % HEAD_DIM
    ang = pos[:, None] * inv[dim % half][None, :]
    cos, sin = jnp.cos(ang), jnp.sin(ang)
    first, second = (dim < half)[None, :], ((dim >= half) & (dim < ROT_DIM))[None, :]
    c = jnp.where(first | second, cos, 1.0)
    s1 = jnp.where(first, -sin, 0.0)
    s2 = jnp.where(second, sin, 0.0)
    return c, s1, s2


def _rope(h, c, s1, s2):
    half = ROT_DIM // 2
    return h * c + pltpu.roll(h, LANES - half, 1) * s1 + pltpu.roll(h, half, 1) * s2


def _to_half(piece, src_half, dst_half, lo):
    if src_half != dst_half:
        piece = pltpu.roll(piece, HALF, 1)
    keep = lo if dst_half == 0 else jnp.logical_not(lo)
    return jnp.where(keep, piece, 0.0)


def _merge_halves(even, odd, src_half, lo):
    if src_half == 1:
        even = pltpu.roll(even, HALF, 1)
    else:
        odd = pltpu.roll(odd, HALF, 1)
    return jnp.where(lo, even, odd)


def _nt(a, b):
    return lax.dot_general(a, b, (((1,), (1,)), ((), ())), preferred_element_type=jnp.float32)


def _band_bias(lo_off, hi_off, no_previous):
    i = lax.broadcasted_iota(jnp.int32, (BLK, 2 * BLK), 0)
    j = lax.broadcasted_iota(jnp.int32, (BLK, 2 * BLK), 1)
    band = (j >= i + lo_off) & (j <= i + hi_off)
    first_key = jnp.where(no_previous, BLK, 0)
    return (jnp.where(band, 0.0, -jnp.inf), jnp.where(band & (j >= first_key), 0.0, -jnp.inf))


def _softmax_pv(s, bias, v_win, sink=None):
    s = s + bias
    m = jnp.max(s, axis=1, keepdims=True)
    if sink is not None:
        m = jnp.maximum(m, sink)
    p = jnp.exp(s - m)
    den = jnp.sum(p, axis=1, keepdims=True)
    if sink is not None:
        den = den + jnp.exp(sink - m)
    o = jnp.dot(p.astype(jnp.bfloat16), v_win, preferred_element_type=jnp.float32)
    return o * (1.0 / den), m + jnp.log(den)


def _in_proj_kernel(x_ref, w_ref, b_ref, c_ref, s1_ref, s2_ref, o_ref, xb_ref):
    j = pl.program_id(1)

    @pl.when(j == 0)
    def _():
        xb_ref[...] = x_ref[...].astype(jnp.bfloat16)

    def tile(rotary):
        for part in range(len(_PARTS)):
            cols = slice(part * _PAIR * LANES, (part + 1) * _PAIR * LANES)
            acc = jnp.dot(xb_ref[...], w_ref[:, cols],
                          preferred_element_type=jnp.float32) + b_ref[:, cols]
            for t in range(_PAIR):
                col = acc[:, t * LANES:(t + 1) * LANES]
                if rotary[part * _PAIR + t]:
                    col = _rope(col, c_ref[...], s1_ref[...], s2_ref[...])
                o_ref[part, t] = col

    patterns = {}
    for tile_index in range(N_TILES):
        patterns.setdefault(_tile_rotary_flags(tile_index), []).append(tile_index)
    for rotary, tiles in patterns.items():
        hit = functools.reduce(jnp.logical_or, [j == t for t in tiles])
        pl.when(hit)(functools.partial(tile, rotary))


def _in_proj(x, w_bf, b, tabs, tm):
    m, k = x.shape
    n_pos_blocks = tabs[0].shape[0] // tm
    tab = pl.BlockSpec((tm, LANES), lambda i, j: (i % n_pos_blocks, 0))
    x_mode = dict(pipeline_mode=pl.Buffered(1)) if tm > 1024 else {}
    return pl.pallas_call(
        _in_proj_kernel,
        grid=(m // tm, N_TILES),
        in_specs=[pl.BlockSpec((tm, k), lambda i, j: (i, 0), **x_mode),
                  pl.BlockSpec((k, _TILE_COLS), lambda i, j: (0, j)),
                  pl.BlockSpec((1, _TILE_COLS), lambda i, j: (0, j)), tab, tab, tab],
        out_specs=pl.BlockSpec((len(_PARTS), _PAIR, tm, LANES), lambda i, j: (0, j, i, 0)),
        out_shape=jax.ShapeDtypeStruct((len(_PARTS), PART_SLABS, m, LANES), jnp.float32),
        scratch_shapes=[pltpu.VMEM((tm, k), jnp.bfloat16)],
        compiler_params=_params(2),
        name="in_proj",
    )(x, w_bf, b, *tabs)


_UNITS_PER_TRIP = 2


def _attend_pairs(slabs, k_wins, v_wins, kv_of_pair, bias, lo, sinks=None):
    result = []
    for p, slab in enumerate(slabs):
        col, half = kv_of_pair[p]
        slab = slab * SCALE
        outs, lses = [], []
        for sub in range(2):
            q_h = _to_half(slab, sub, half, lo).astype(jnp.bfloat16)
            o, lse = _softmax_pv(_nt(q_h, k_wins[col]), bias, v_wins[col],
                                 None if sinks is None else sinks[2 * p + sub])
            outs.append(o), lses.append(lse)
        result.append((_merge_halves(outs[0], outs[1], half, lo),
                       jnp.where(lo, jnp.broadcast_to(lses[0], (BLK, LANES)),
                                 jnp.broadcast_to(lses[1], (BLK, LANES)))))
    return result


def _attn_a_kernel(sink_ref, q_ref, kv_ref, kvp_ref, o_ref, k_buf, v_buf, *, n_sub):
    lo = lax.broadcasted_iota(jnp.int32, (BLK, LANES), 1) < HALF
    seq_start = pl.program_id(1) == 0
    group = A_Q_HEADS // A_KV_HEADS
    for buf, c in ((k_buf, 0), (v_buf, 1)):
        buf[0:BLK, :] = kvp_ref[c].astype(jnp.bfloat16)
        buf[BLK:, :] = kv_ref[c].astype(jnp.bfloat16)
    sinks = [sink_ref[h] for h in range(A_Q_HEADS)]
    kv_of_pair = [(0, (2 * p) // group) for p in range(A_Q_HEADS // 2)]

    def block(j, _):
        start = pl.multiple_of(j * BLK, BLK)
        rows = pl.ds(start, BLK)
        window = pl.ds(start, 2 * BLK)
        _, bias = _band_bias(1, A_WINDOW, seq_start & (j == 0))
        outs = _attend_pairs([q_ref[p, rows, :] for p in range(A_Q_HEADS // 2)],
                             [k_buf[window, :]], [v_buf[window, :]], kv_of_pair, bias, lo, sinks)
        for p, (o, _) in enumerate(outs):
            o_ref[p, rows, :] = o

    lax.fori_loop(0, n_sub, block, None)


def _attn_a(h3, sink, n, l, n_sub=8):
    rows = BLK * n_sub
    nsb = l // rows
    q_slabs = A_WIDTH // LANES
    tok = lambda i, sb: i * nsb + sb
    prev_blk = lambda i, sb: jnp.maximum(tok(i, sb) * n_sub - 1, 0)
    return pl.pallas_call(
        functools.partial(_attn_a_kernel, n_sub=n_sub),
        grid=(n, nsb),
        in_specs=[pl.BlockSpec(memory_space=pltpu.SMEM),
                  _seg_spec("qa", q_slabs, rows, tok),
                  _seg_spec("ka", 2, rows, tok),
                  _seg_spec("ka", 2, BLK, prev_blk)],
        out_specs=pl.BlockSpec((q_slabs, rows, LANES), lambda i, sb: (0, tok(i, sb), 0)),
        out_shape=jax.ShapeDtypeStruct((q_slabs, n * l, LANES), jnp.float32),
        scratch_shapes=[pltpu.VMEM((rows + BLK, LANES), jnp.bfloat16)] * 2,
        compiler_params=_params(2),
        name="attn_a",
    )(sink, h3, h3, h3)


def _attn_b_kernel(q_ref, k_ref, v_ref, kp_ref, vp_ref, o_ref, lse_ref, k_buf, v_buf, *,
                   dil, n_sub):
    n_col = B_KV_WIDTH // LANES
    lo = lax.broadcasted_iota(jnp.int32, (BLK, LANES), 1) < HALF
    seq_start = pl.program_id(1) == 0
    kv_of_pair = [(g // 2, g % 2) for g in range(B_KV_HEADS)]

    def fold(r, j):
        if dil == 1:
            return pl.ds(pl.multiple_of(j * BLK, BLK), BLK)
        return pl.ds(j * (BLK * dil) + r, BLK, stride=dil)

    def fill(r, _):
        for buf, prev, cur in ((k_buf, kp_ref, k_ref), (v_buf, vp_ref, v_ref)):
            for c in range(n_col):
                buf[r, c, 0:BLK, :] = prev[c, fold(r, 0), :].astype(jnp.bfloat16)
                for j in range(n_sub):
                    buf[r, c, (j + 1) * BLK:(j + 2) * BLK, :] = (
                        cur[c, fold(r, j), :].astype(jnp.bfloat16))

    def block(u, _):
        r, j = u // n_sub, u % n_sub
        window = pl.ds(pl.multiple_of(j * BLK, BLK), 2 * BLK)
        _, bias = _band_bias(0, BLK, seq_start & (j == 0))
        outs = _attend_pairs([q_ref[g, fold(r, j), :] for g in range(B_KV_HEADS)],
                             [k_buf[r, c, window, :] for c in range(n_col)],
                             [v_buf[r, c, window, :] for c in range(n_col)], kv_of_pair, bias, lo)
        for g, (o, lse) in enumerate(outs):
            o_ref[g, fold(r, j), :] = o
            lse_ref[g, fold(r, j), :] = lse

    if dil == 1:
        fill(0, None)
    else:
        lax.fori_loop(0, dil, fill, None)
    lax.fori_loop(0, dil * n_sub, block, None, unroll=_UNITS_PER_TRIP)


def _attn_b(h3, gi, dil, n, l, n_sub):
    prev_rows = BLK * dil
    rows = prev_rows * n_sub
    nsb = l // rows
    q_slabs = B_WIDTH // LANES
    kv_slabs = B_KV_WIDTH // LANES
    tok = lambda i, sb: i * nsb + sb
    prev_blk = lambda i, sb: jnp.maximum(tok(i, sb) * n_sub - 1, 0)
    out_blk = pl.BlockSpec((q_slabs, rows, LANES), lambda i, sb: (0, tok(i, sb), 0))

    def kv_specs(name):
        return (_seg_spec(f"{name}{gi}", kv_slabs, rows, tok),
                _seg_spec(f"{name}{gi}", kv_slabs, prev_rows, prev_blk))

    (k_cur, k_prev), (v_cur, v_prev) = kv_specs("kb"), kv_specs("vb")
    return pl.pallas_call(
        functools.partial(_attn_b_kernel, dil=dil, n_sub=n_sub),
        grid=(n, nsb),
        in_specs=[_seg_spec(f"qb{gi}", q_slabs, rows, tok), k_cur, v_cur, k_prev, v_prev],
        out_specs=[out_blk, out_blk],
        out_shape=[jax.ShapeDtypeStruct((q_slabs, n * l, LANES), jnp.float32)] * 2,
        scratch_shapes=[pltpu.VMEM((dil, kv_slabs, (n_sub + 1) * BLK, LANES), jnp.bfloat16)] * 2,
        compiler_params=_params(2),
        name=f"attn_b{gi}",
    )(h3, h3, h3, h3, h3)


def _kv_state_kernel(k_ref, v_ref, o_ref):
    n_k = k_ref.shape[0]
    for c in range(n_k):
        o_ref[0, c * LANES:(c + 1) * LANES, :] = jnp.transpose(k_ref[c])
        o_ref[0, (n_k + c) * LANES:(n_k + c + 1) * LANES, :] = jnp.transpose(v_ref[c])


def _kv_state(h3, k_name, v_name, width, win, n, l, name):
    slabs = width // LANES
    chunk = min(win, 4 * BLK)
    per_seq = win // chunk
    first_chunk = lambda i, t: ((i + 1) * l - win) // chunk + t
    spec = lambda s: _seg_spec(s, slabs, chunk, first_chunk)
    return pl.pallas_call(
        _kv_state_kernel,
        grid=(n, per_seq),
        in_specs=[spec(k_name), spec(v_name)],
        out_specs=pl.BlockSpec((1, 2 * width, chunk), lambda i, t: (i, 0, t)),
        out_shape=jax.ShapeDtypeStruct((n, 2 * width, win), jnp.float32),
        compiler_params=_params(2),
        name=name,
    )(h3, h3)


_PAD = 16
_SAMPLE_SEQS_PER_STEP = 2


def _pad_rows(x, rows=_PAD, at=0):
    parts = []
    if at:
        parts.append(jnp.zeros((at, x.shape[1]), x.dtype))
    parts.append(x)
    if rows - at - x.shape[0]:
        parts.append(jnp.zeros((rows - at - x.shape[0], x.shape[1]), x.dtype))
    return jnp.concatenate(parts, axis=0)


def _sample_bias(shape, first_key, cb, dil, max_dist):
    t = lax.broadcasted_iota(jnp.int32, shape, 0) & (DEC_SEQ - 1)
    dist = cb + t - (lax.broadcasted_iota(jnp.int32, shape, 1) + first_key)
    valid = (dist >= 0) & (dist <= max_dist) & ((dist & (dil - 1)) == 0)
    return jnp.where(valid, 0.0, -jnp.inf)


def _sample_attend(qz, cache_ref, b, bias_ref, k_new, v_new, width, dil, max_dist, sink=None):
    cb = cache_ref.shape[2]
    qz = qz.astype(jnp.bfloat16)
    s_c = jnp.dot(qz, cache_ref[b, 0:width, :].astype(jnp.bfloat16),
                  preferred_element_type=jnp.float32) + bias_ref[...]
    s_n = _nt(qz, _pad_rows(k_new).astype(jnp.bfloat16))
    s_n = s_n + _sample_bias(s_n.shape, cb, cb, dil, max_dist)
    m = jnp.maximum(jnp.max(s_c, axis=1, keepdims=True), jnp.max(s_n, axis=1, keepdims=True))
    if sink is not None:
        m = jnp.maximum(m, sink)
    p_c, p_n = jnp.exp(s_c - m), jnp.exp(s_n - m)
    den = jnp.sum(p_c, axis=1, keepdims=True) + jnp.sum(p_n, axis=1, keepdims=True)
    if sink is not None:
        den = den + jnp.exp(sink - m)
    o = (_nt(p_c.astype(jnp.bfloat16), cache_ref[b, width:2 * width, :].astype(jnp.bfloat16))
         + jnp.dot(p_n.astype(jnp.bfloat16), _pad_rows(v_new).astype(jnp.bfloat16),
                   preferred_element_type=jnp.float32))
    return o / den, m + jnp.log(den)


def _shift_cache(cache_ref, new_ref, b, new_rows):
    cb = cache_ref.shape[2]
    lane = lax.broadcasted_iota(jnp.int32, (cache_ref.shape[1], LANES), 1)
    keep = lane < LANES - DEC_SEQ
    tail = jnp.transpose(_pad_rows(new_rows, LANES, LANES - DEC_SEQ))
    cur = pltpu.roll(cache_ref[b, :, 0:LANES], LANES - DEC_SEQ, 1)
    for j in range(cb // LANES):
        if j + 1 < cb // LANES:
            nxt = pltpu.roll(cache_ref[b, :, (j + 1) * LANES:(j + 2) * LANES], LANES - DEC_SEQ, 1)
        else:
            nxt = tail
        new_ref[b, :, j * LANES:(j + 1) * LANES] = jnp.where(keep, cur, nxt)
        cur = nxt


def _sample_kernel(*refs):
    cache_refs, bias_refs = refs[1:5], refs[-4:]

    @pl.when(pl.program_id(0) == 0)
    def _():
        windows = [(1, A_WINDOW - 1)] + [(dil, win) for win, dil in B_PAIRS]
        for cache, bias, (dil, max_dist) in zip(cache_refs, bias_refs, windows):
            bias[...] = _sample_bias(bias.shape, 0, cache.shape[2], dil, max_dist)

    for b in range(_SAMPLE_SEQS_PER_STEP):
        _sample_one(b, *refs)


def _sample_one(b, h_ref, ca_ref, c0_ref, c1_ref, c2_ref, sink_ref,
                oa_ref, ob_ref, na_ref, n0_ref, n1_ref, n2_ref, ba_ref, b0_ref, b1_ref, b2_ref):
    lo = lax.broadcasted_iota(jnp.int32, (DEC_SEQ, LANES), 1) < HALF
    rows = slice(b * DEC_SEQ, (b + 1) * DEC_SEQ)
    slabs = lambda name, count: [h_ref[_SEG[name][0], _SEG[name][1] + c, rows, :]
                                 for c in range(count)]

    group = A_Q_HEADS // A_KV_HEADS
    pieces = []
    for pair, slab in enumerate(slabs("qa", A_WIDTH // LANES)):
        for sub in range(2):
            pieces.append(_to_half(slab * SCALE, sub, (2 * pair + sub) // group, lo))
    (k_new,), (v_new,) = slabs("ka", 1), slabs("va", 1)
    o, _ = _sample_attend(jnp.concatenate(pieces, axis=0), ca_ref, b, ba_ref, k_new, v_new,
                          A_KV_WIDTH, 1, A_WINDOW - 1, sink_ref[...])
    for pair in range(A_Q_HEADS // 2):
        oa_ref[pair, rows, :] = _merge_halves(o[(2 * pair) * DEC_SEQ:(2 * pair + 1) * DEC_SEQ, :],
                                     o[(2 * pair + 1) * DEC_SEQ:(2 * pair + 2) * DEC_SEQ, :],
                                     (2 * pair) // group, lo)
    _shift_cache(ca_ref, na_ref, b, jnp.concatenate([k_new, v_new], axis=1))

    n_col = B_KV_WIDTH // LANES
    zeros = jnp.zeros((DEC_SEQ, LANES), jnp.float32)
    outs, lses = [], []
    for gi, ((win, dil), c_ref, n_ref, bias_ref) in enumerate(
            zip(B_PAIRS, (c0_ref, c1_ref, c2_ref), (n0_ref, n1_ref, n2_ref),
                (b0_ref, b1_ref, b2_ref))):
        pieces = []
        for g, slab in enumerate(slabs(f"qb{gi}", B_WIDTH // LANES)):
            for sub in range(2):
                cols = [zeros] * n_col
                cols[g // 2] = _to_half(slab * SCALE, sub, g % 2, lo)
                pieces.append(jnp.concatenate(cols, axis=1))
        k_new = jnp.concatenate(slabs(f"kb{gi}", n_col), axis=1)
        v_new = jnp.concatenate(slabs(f"vb{gi}", n_col), axis=1)
        o, lse = _sample_attend(jnp.concatenate(pieces, axis=0), c_ref, b, bias_ref, k_new, v_new,
                                B_KV_WIDTH, dil, win)
        outs.append(o), lses.append(lse)
        _shift_cache(c_ref, n_ref, b, jnp.concatenate([k_new, v_new], axis=1))
    top = functools.reduce(jnp.maximum, lses)
    es = [jnp.exp(v - top) for v in lses]
    tot = functools.reduce(lambda a, c: a + c, es)
    o = functools.reduce(lambda a, c: a + c, [(e / tot) * v for e, v in zip(es, outs)])
    for g in range(B_KV_HEADS):
        col = slice((g // 2) * LANES, (g // 2 + 1) * LANES)
        ob_ref[g, rows, :] = _merge_halves(o[(2 * g) * DEC_SEQ:(2 * g + 1) * DEC_SEQ, col],
                                  o[(2 * g + 1) * DEC_SEQ:(2 * g + 2) * DEC_SEQ, col], g % 2, lo)


def _sample_step(hs3, caches_t, sink_rows):
    n = caches_t[0].shape[0]
    per_step = _SAMPLE_SEQS_PER_STEP
    cache_specs = [pl.BlockSpec((per_step,) + c.shape[1:], lambda i: (i, 0, 0)) for c in caches_t]
    row_blk = lambda slabs: pl.BlockSpec((slabs, per_step * DEC_SEQ, LANES), lambda i: (0, i, 0))
    return pl.pallas_call(
        _sample_kernel,
        grid=(n // per_step,),
        in_specs=[pl.BlockSpec((len(_PARTS), PART_SLABS, per_step * DEC_SEQ, LANES),
                               lambda i: (0, 0, i, 0))] + cache_specs
        + [pl.BlockSpec((A_Q_HEADS * DEC_SEQ, 1), lambda i: (0, 0))],
        out_specs=[row_blk(A_WIDTH // LANES), row_blk(B_WIDTH // LANES)] + cache_specs,
        out_shape=[jax.ShapeDtypeStruct((A_WIDTH // LANES, n * DEC_SEQ, LANES), jnp.float32),
                   jax.ShapeDtypeStruct((B_WIDTH // LANES, n * DEC_SEQ, LANES), jnp.float32)]
        + [jax.ShapeDtypeStruct(c.shape, jnp.float32) for c in caches_t],
        scratch_shapes=[pltpu.VMEM((rows * DEC_SEQ, c.shape[2]), jnp.float32)
                        for rows, c in zip([A_Q_HEADS] + [B_Q_HEADS] * 3, caches_t)],
        compiler_params=_params(1),
        name="sample_step",
    )(hs3, *caches_t, sink_rows)


def _sigmoid(x):
    return 1.0 / (1.0 + jnp.exp(-x))


def _out_kernel(*refs, n_groups):
    x_ref, oa_ref, za_ref, ga_ref, gb_ref, zb_ref = refs[:6]
    ob_refs = refs[6:6 + n_groups]
    lse_refs = refs[6 + n_groups:6 + 2 * n_groups] if n_groups > 1 else ()
    wa_ref, wb_ref, wo_ref, lng_ref, lnb_ref, y_ref = refs[-6:]

    def gated(o, z):
        return (o * (z * _sigmoid(z))).astype(jnp.bfloat16)

    ya = jnp.concatenate([gated(oa_ref[s], za_ref[s]) for s in range(A_WIDTH // LANES)], axis=1)
    yb = []
    for s in range(B_WIDTH // LANES):
        if n_groups > 1:
            lses = [r[s] for r in lse_refs]
            top = functools.reduce(jnp.maximum, lses)
            es = [jnp.exp(v - top) for v in lses]
            tot = functools.reduce(lambda a, c: a + c, es)
            ob = functools.reduce(lambda a, c: a + c,
                                  [(e / tot) * r[s] for e, r in zip(es, ob_refs)])
        else:
            ob = ob_refs[0][s]
        yb.append(gated(ob, zb_ref[s]))
    yb = jnp.concatenate(yb, axis=1)
    da = jnp.dot(ya, wa_ref[...], preferred_element_type=jnp.float32)
    db = jnp.dot(yb, wb_ref[...], preferred_element_type=jnp.float32)
    m = jnp.concatenate(
        [(_sigmoid(ga_ref[s]) * da[:, s * LANES:(s + 1) * LANES]
          + _sigmoid(gb_ref[s]) * db[:, s * LANES:(s + 1) * LANES]).astype(jnp.bfloat16)
         for s in range(D_MODEL // LANES)], axis=1)
    h = DN_ALPHA * x_ref[...] + jnp.dot(m, wo_ref[...], preferred_element_type=jnp.float32)
    mu = jnp.mean(h, axis=-1, keepdims=True)
    var = jnp.mean(jnp.square(h - mu), axis=-1, keepdims=True)
    y_ref[...] = (h - mu) * lax.rsqrt(var + LN_EPS) * lng_ref[...] + lnb_ref[...]


def _out_proj(x, h3, oa, obs, lses, wa, wb, wo, ln_g, ln_b, tm=256):
    m = x.shape[0]
    n_groups = len(obs)

    mixer_out = lambda count: pl.BlockSpec((count, tm, LANES), lambda i: (0, i, 0))
    seg = lambda name, count: _seg_spec(name, count, tm, lambda i: i)
    row = pl.BlockSpec((tm, D_MODEL), lambda i: (i, 0))
    whole = lambda a: pl.BlockSpec(a.shape, lambda i: (0, 0), pipeline_mode=pl.Buffered(1))
    a_slabs, b_slabs, d_slabs = A_WIDTH // LANES, B_WIDTH // LANES, D_MODEL // LANES
    in_specs = ([row, mixer_out(a_slabs), seg("za", a_slabs), seg("ga", d_slabs),
                 seg("gb", d_slabs), seg("zb", b_slabs)]
                + [mixer_out(b_slabs)] * (len(obs) + len(lses))
                + [whole(wa), whole(wb), whole(wo), whole(ln_g), whole(ln_b)])
    return pl.pallas_call(
        functools.partial(_out_kernel, n_groups=n_groups),
        grid=(m // tm,),
        in_specs=in_specs,
        out_specs=row,
        out_shape=jax.ShapeDtypeStruct((m, D_MODEL), jnp.float32),
        compiler_params=_params(1),
        name=f"out_proj{n_groups}",
    )(x, oa, h3, h3, h3, h3, *obs, *lses, wa, wb, wo, ln_g, ln_b)


_UNIT = _PAIR * LANES
assert all(_COL_ORDER[u * _UNIT] % _UNIT == 0
           and np.all(np.diff(_COL_ORDER[u * _UNIT:(u + 1) * _UNIT]) == 1)
           for u in range(IN_WIDTH // _UNIT))


def _cast_permute_kernel(order_ref, *refs):
    del order_ref
    o_ref = refs[-1]
    for u, w_ref in enumerate(refs[:-1]):
        o_ref[:, u * _UNIT:(u + 1) * _UNIT] = w_ref[...].astype(jnp.bfloat16)


def _cast_permute_w(w):
    k, n = w.shape
    per_step = len(_PARTS)
    unit_order = jnp.asarray(_COL_ORDER[::_UNIT] // _UNIT, jnp.int32)
    unit_spec = lambda u: pl.BlockSpec((k, _UNIT), lambda j, order: (0, order[j * per_step + u]))
    return pl.pallas_call(
        _cast_permute_kernel,
        grid_spec=pltpu.PrefetchScalarGridSpec(
            num_scalar_prefetch=1, grid=(n // (_UNIT * per_step),),
            in_specs=[unit_spec(u) for u in range(per_step)],
            out_specs=pl.BlockSpec((k, _UNIT * per_step), lambda j, order: (0, j))),
        out_shape=jax.ShapeDtypeStruct((k, n), jnp.bfloat16),
        compiler_params=_params(1),
        name="cast_permute_w",
    )(unit_order, *([w] * per_step))


def kernel(x_prompt, x_sample, cache_a_kv, cache_b0_kv, cache_b1_kv, cache_b2_kv,
           w_in, b_in, sink_a, w_br_a, w_br_b, w_out, ln_g, ln_b):
    n, l, _ = x_prompt.shape
    ns, t, _ = x_sample.shape
    assert w_in.shape[0] == 1 and t == DEC_SEQ and l == SEQ
    assert PAST_LEN >= max(w for w, _ in B_PAIRS)
    w_bf = _cast_permute_w(w_in[0])
    b_p = jnp.take(b_in[0], jnp.asarray(_COL_ORDER, jnp.int32))[None, :]
    wa, wb, wo = (w[0].astype(jnp.bfloat16) for w in (w_br_a, w_br_b, w_out))
    tabs_p = _rope_tables(jnp.arange(l, dtype=jnp.float32))
    tabs_s = [jnp.tile(tb, (ns, 1)) for tb in
              _rope_tables(PAST_LEN + jnp.arange(t, dtype=jnp.float32))]
    sink = sink_a[0]
    sink_rows = jnp.repeat(sink, DEC_SEQ)[:, None]

    xp = x_prompt.reshape(n * l, D_MODEL)
    hp = _in_proj(xp, w_bf, b_p, tabs_p, tm=2048)
    oa = _attn_a(hp, sink, n, l)
    st_a = _kv_state(hp, "ka", "va", A_KV_WIDTH, min(A_WINDOW, l), n, l, "state_a")
    obs, lses, st_b = [], [], []
    for gi, (win, dil) in enumerate(B_PAIRS):
        o, lse = _attn_b(hp, gi, dil, n, l, n_sub={1: 8, 4: 4, 16: 1}[dil])
        obs.append(o), lses.append(lse)
        st_b.append(_kv_state(hp, f"kb{gi}", f"vb{gi}", B_KV_WIDTH, min(win, l), n, l,
                              f"state_b{gi}"))
    yp = _out_proj(xp, hp, oa, obs, lses, wa, wb, wo, ln_g, ln_b)

    xs = x_sample.reshape(ns * t, D_MODEL)
    hs = _in_proj(xs, w_bf, b_p, tabs_s, tm=1024)
    to_t = lambda c, w: jnp.transpose(c[0].reshape(ns, -1, w), (0, 2, 1))
    caches_t = [to_t(cache_a_kv, 2 * A_KV_WIDTH)] + [
        to_t(c, 2 * B_KV_WIDTH) for c in (cache_b0_kv, cache_b1_kv, cache_b2_kv)]
    oa_s, ob_s, *new_t = _sample_step(hs, caches_t, sink_rows)
    ys = _out_proj(xs, hs, oa_s, [ob_s], [], wa, wb, wo, ln_g, ln_b)

    kv_shape = lambda a, heads: a.reshape((1, a.shape[0], a.shape[1], 2, heads, HEAD_DIM))
    from_t = lambda a, heads: kv_shape(jnp.transpose(a, (0, 2, 1)), heads)
    return (yp.reshape(n, l, D_MODEL), ys.reshape(ns, t, D_MODEL),
            from_t(st_a, A_KV_HEADS), *[from_t(s, B_KV_HEADS) for s in st_b],
            from_t(new_t[0], A_KV_HEADS), *[from_t(c, B_KV_HEADS) for c in new_t[1:]])
```
